```python
import math
import jax, jax.numpy as jnp
from jax import lax
import numpy as np

D_MODEL = 1024
BATCH = 4
SEQ = 4096
DEPTH = 4
DEC_BATCH = 128
DEC_SEQ = 1
PAST_LEN = 8192
PAGE_SIZE = 128

N_HEADS = 16
HEAD_DIM = 64
N_KV = 2
GROUP = N_HEADS // N_KV
HD = N_HEADS * HEAD_DIM
KVD = N_KV * HEAD_DIM
D_FF = 4 * D_MODEL
WIN_A = 128
WIN_B = 512
Q_BLOCK = 128
CMP_STRIDE = 16
CMP_LEN = 2 * CMP_STRIDE
CMP_HIDDEN = 128
SEL_BLOCK = 64
N_SELECT = 16
FORCED_SCORE = 1e9
N_BUCKETS = 32
MAX_DISTANCE = 128
NORM_EPS = 1e-5
N_LAYERS_A = (DEPTH + 1) // 2
N_LAYERS_B = DEPTH // 2
IN_A = HD + 2 * KVD
IN_B = HD + 6 * KVD + 3 * N_HEADS
SCALE = HEAD_DIM ** -0.5

kernel_name = 'hybrid_swa_sink_nsa_decoder_step'


def rmsnorm(x, g):
    xf = x.astype(jnp.float32)
    y = xf * lax.rsqrt(jnp.mean(xf * xf, axis=-1, keepdims=True) + NORM_EPS)
    return (y * g.astype(jnp.float32)).astype(x.dtype)


def sqrelu_mlp(x, w_up, w_down):
    h = jax.nn.relu(x @ w_up)
    return (h * h) @ w_down


def rel_bucket(dist):
    n = jnp.maximum(dist, 0)
    exact = N_BUCKETS // 2
    nf = jnp.maximum(n, exact).astype(jnp.float32)
    large = exact + (jnp.log(nf / exact) / math.log(MAX_DISTANCE / exact) * (N_BUCKETS - exact)).astype(jnp.int32)
    return jnp.where(n < exact, n, jnp.minimum(large, N_BUCKETS - 1))


def head_bias(rel_bias, dist):
    b = rel_bias.astype(jnp.float32)[rel_bucket(dist)]
    return jnp.moveaxis(b, -1, 0).reshape((N_KV, GROUP) + dist.shape)


def masked_softmax(s, mask, sink=None):
    s = jnp.where(mask, s, -jnp.inf)
    m = jnp.max(s, axis=-1, keepdims=True)
    if sink is not None:
        m = jnp.maximum(m, sink)
    m = jnp.where(jnp.isfinite(m), m, 0.0)
    e = jnp.exp(s - m)
    den = jnp.sum(e, axis=-1, keepdims=True)
    if sink is not None:
        den = den + jnp.exp(sink - m)
    return e / jnp.where(den > 0, den, 1.0)


def attend(q, k, v, mask, bias, sink=None):
    s = jnp.einsum('bqgrd,bkgd->bgrqk', q, k).astype(jnp.float32) * SCALE + bias
    sk = None if sink is None else sink.astype(jnp.float32).reshape(N_KV, GROUP, 1, 1)
    p = masked_softmax(s, mask[:, None, None], sk)
    o = jnp.einsum('bgrqk,bkgd->bqgrd', p.astype(v.dtype), v)
    return o, p


def banded_attention(q, k, v, window, rel_bias, sink=None):
    B, T = q.shape[:2]
    nblk = T // Q_BLOCK
    nback = -(-window // Q_BLOCK)
    pad = nback * Q_BLOCK
    kb_len = (nback + 1) * Q_BLOCK

    def band(a):
        ap = jnp.pad(a, ((0, 0), (pad, 0), (0, 0), (0, 0))).reshape(B, nblk + nback, Q_BLOCK, N_KV, HEAD_DIM)
        return jnp.concatenate([ap[:, o:o + nblk] for o in range(nback + 1)], axis=2).reshape(B * nblk, kb_len, N_KV, HEAD_DIM)

    qb = q.reshape(B * nblk, Q_BLOCK, N_KV, GROUP, HEAD_DIM)
    qi = jnp.arange(Q_BLOCK)
    kj = jnp.arange(kb_len)
    dist = pad + qi[:, None] - kj[None, :]
    kpos = (jnp.arange(nblk)[:, None] - nback) * Q_BLOCK + kj[None, :]
    mask = ((dist >= 0) & (dist < window))[None] & (kpos >= 0)[:, None, :]
    mask = jnp.broadcast_to(mask[None], (B, nblk, Q_BLOCK, kb_len)).reshape(B * nblk, Q_BLOCK, kb_len)
    o, _ = attend(qb, band(k), band(v), mask, head_bias(rel_bias, dist), sink)
    return o.reshape(B, T, N_KV, GROUP, HEAD_DIM)


def window_decode(q, k_new, v_new, buf, window, rel_bias, sink=None):
    lb = buf.shape[1]
    S = q.shape[1]
    k = jnp.concatenate([buf[:, :, 0], k_new], axis=1)
    v = jnp.concatenate([buf[:, :, 1], v_new], axis=1)
    qpos = PAST_LEN + jnp.arange(S)
    kpos = PAST_LEN - lb + jnp.arange(lb + S)
    dist = qpos[:, None] - kpos[None, :]
    mask = ((dist >= 0) & (dist < window))[None]
    o, _ = attend(q, k, v, mask, head_bias(rel_bias, dist), sink)
    return o, jnp.stack([k[:, S:], v[:, S:]], axis=2)


def compress(kv, w1, w2, pe):
    B, T = kv.shape[:2]
    nch = T // CMP_STRIDE
    c = kv[:, :nch * CMP_STRIDE].reshape(B, nch, CMP_STRIDE, N_KV, HEAD_DIM)
    lo = jnp.einsum('bcpgd,pde->bcge', c, w1[:CMP_STRIDE])[:, :-1]
    hi = jnp.einsum('bcpgd,pde->bcge', c, w1[CMP_STRIDE:])[:, 1:]
    h = lo + hi + jnp.einsum('pd,pde->e', pe, w1)
    return jnp.einsum('bcge,ed->bcgd', jax.nn.silu(h), w2)


def compressed_branch(q, kc, vc, w1, w2, pe, qpos, rel_bias):
    Kc = compress(kc, w1[0], w2[0], pe[0])
    Vc = compress(vc, w1[1], w2[1], pe[1])
    blk_end = jnp.arange(Kc.shape[1]) * CMP_STRIDE + CMP_LEN - 1
    dist = qpos[:, None] - blk_end[None, :]
    return attend(q, Kc, Vc, (dist >= 0)[None], head_bias(rel_bias, dist))


def select_blocks(p_cmp, qpos, n_blocks):
    nc = p_cmp.shape[-1]
    ci = jnp.arange(nc) * CMP_STRIDE
    sj = jnp.arange(n_blocks) * SEL_BLOCK
    overlap = ((ci[:, None] < sj[None, :] + SEL_BLOCK) & (ci[:, None] + CMP_LEN > sj[None, :])).astype(p_cmp.dtype)
    imp = jnp.einsum('bgrqc,cj->bqgj', p_cmp, overlap)
    j = jnp.arange(n_blocks)
    cur = (qpos // SEL_BLOCK)[:, None, None]
    forced = (j == 0) | (j == cur) | (j == cur - 1)
    score = jnp.where(j <= cur, jnp.where(forced, FORCED_SCORE, imp), -jnp.inf)
    vals, idx = lax.top_k(score, min(N_SELECT, n_blocks))
    return idx, jnp.isfinite(vals)


def selected_attend(q, kg, vg, idx, valid, qpos, rel_bias):
    kpos = idx[..., None] * SEL_BLOCK + jnp.arange(SEL_BLOCK)
    dist = qpos[:, None, None, None] - kpos
    mask = (dist >= 0) & valid[..., None]
    tbl = rel_bias.astype(jnp.float32).reshape(N_BUCKETS, N_KV, GROUP).transpose(1, 0, 2)
    bias = jnp.moveaxis(tbl[jnp.arange(N_KV)[:, None, None], rel_bucket(dist)], -1, 3)
    s = jnp.einsum('bqgrd,bqgnld->bqgrnl', q, kg).astype(jnp.float32) * SCALE + bias
    B, Q, G, R, n, L = s.shape
    p = masked_softmax(s.reshape(B, Q, G, R, n * L), mask[:, :, :, None].reshape(B, Q, G, 1, n * L))
    return jnp.einsum('bqgrnl,bqgnld->bqgrd', p.reshape(s.shape).astype(vg.dtype), vg)


def selected_prompt(q, ks, vs, idx, valid, rel_bias):
    B, T = q.shape[:2]
    nb = T // SEL_BLOCK
    kb = ks.reshape(B, nb, SEL_BLOCK, N_KV, HEAD_DIM).transpose(0, 3, 1, 2, 4)
    vb = vs.reshape(B, nb, SEL_BLOCK, N_KV, HEAD_DIM).transpose(0, 3, 1, 2, 4)
    nq = T // Q_BLOCK
    bi = jnp.arange(B)[:, None, None, None]
    gi = jnp.arange(N_KV)[None, None, :, None]

    def split(a):
        return jnp.moveaxis(a.reshape((B, nq, Q_BLOCK) + a.shape[2:]), 1, 0)

    def one_block(args):
        qb, ib, okb, pb = args
        return selected_attend(qb, kb[bi, gi, ib], vb[bi, gi, ib], ib, okb, pb, rel_bias)

    pos = jnp.arange(T).reshape(nq, Q_BLOCK)
    o = lax.map(one_block, (split(q), split(idx), split(valid), pos))
    return jnp.moveaxis(o, 0, 1).reshape(B, T, N_KV, GROUP, HEAD_DIM)


def selected_decode(q, ks_new, vs_new, pool, page_table, layer, idx, valid, qpos, rel_bias):
    DB, S = q.shape[:2]
    bpp = PAGE_SIZE // SEL_BLOCK
    n_past = PAST_LEN // SEL_BLOCK
    n_new = -(-S // SEL_BLOCK)
    padn = n_new * SEL_BLOCK - S

    def new_blocks(a):
        a = jnp.pad(a, ((0, 0), (0, padn), (0, 0), (0, 0)))
        return a.reshape(DB, n_new, SEL_BLOCK, N_KV, HEAD_DIM).transpose(0, 3, 1, 2, 4)

    bi = jnp.arange(DB)[:, None, None, None]
    gi = jnp.arange(N_KV)[None, None, :, None]
    is_past = (idx < n_past)[..., None, None]
    pidx = jnp.minimum(idx, n_past - 1)
    phys = page_table[bi, pidx // bpp]
    rows = (pidx % bpp)[..., None] * SEL_BLOCK + jnp.arange(SEL_BLOCK)
    g5 = jnp.arange(N_KV)[:, None, None]
    nidx = jnp.clip(idx - n_past, 0, n_new - 1)
    kg = jnp.where(is_past, pool[phys[..., None], rows, layer, 2, g5], new_blocks(ks_new)[bi, gi, nidx])
    vg = jnp.where(is_past, pool[phys[..., None], rows, layer, 3, g5], new_blocks(vs_new)[bi, gi, nidx])
    return selected_attend(q, kg, vg, idx, valid, qpos, rel_bias)


def swa_project(xn, w_in, b_in):
    B, T, _ = xn.shape
    u = xn @ w_in + b_in
    q = u[..., :HD].reshape(B, T, N_KV, GROUP, HEAD_DIM)
    k = u[..., HD:HD + KVD].reshape(B, T, N_KV, HEAD_DIM)
    v = u[..., HD + KVD:].reshape(B, T, N_KV, HEAD_DIM)
    return q, k, v


def swa_prompt(xn, w_in, b_in, w_out, b_out, sinks, rel_bias):
    B, T, _ = xn.shape
    q, k, v = swa_project(xn, w_in, b_in)
    o = banded_attention(q, k, v, WIN_A, rel_bias, sinks)
    y = o.reshape(B, T, HD) @ w_out + b_out
    return y, jnp.stack([k, v], axis=2)[:, T - min(WIN_A, T):]


def swa_decode(xn, buf, w_in, b_in, w_out, b_out, sinks, rel_bias):
    B, S, _ = xn.shape
    q, k, v = swa_project(xn, w_in, b_in)
    o, new_buf = window_decode(q, k, v, buf, WIN_A, rel_bias, sinks)
    return o.reshape(B, S, HD) @ w_out + b_out, new_buf


def nsa_project(xn, w_in):
    B, T, _ = xn.shape
    u = xn @ w_in
    q = u[..., :HD].reshape(B, T, N_KV, GROUP, HEAD_DIM)
    kv = u[..., HD:HD + 6 * KVD].reshape(B, T, 6, N_KV, HEAD_DIM)
    g = jax.nn.sigmoid(u[..., HD + 6 * KVD:].astype(jnp.float32)).reshape(B, T, 3, N_KV, GROUP, 1).astype(xn.dtype)
    return q, kv, g


def nsa_merge(g, o_c, o_s, o_w, w_out):
    B, T = o_c.shape[:2]
    o = g[:, :, 0] * o_c + g[:, :, 1] * o_s + g[:, :, 2] * o_w
    return o.reshape(B, T, HD) @ w_out


def nsa_prompt(xn, w_in, w_out, w_cmp1, w_cmp2, pe_cmp, rel_bias):
    B, T, _ = xn.shape
    q, kv, g = nsa_project(xn, w_in)
    qpos = jnp.arange(T)
    o_c, p = compressed_branch(q, kv[:, :, 0], kv[:, :, 1], w_cmp1, w_cmp2, pe_cmp, qpos, rel_bias)
    idx, valid = select_blocks(p, qpos, T // SEL_BLOCK)
    o_s = selected_prompt(q, kv[:, :, 2], kv[:, :, 3], idx, valid, rel_bias)
    o_w = banded_attention(q, kv[:, :, 4], kv[:, :, 5], WIN_B, rel_bias)
    y = nsa_merge(g, o_c, o_s, o_w, w_out)
    return y, kv[:, :, :4], kv[:, T - min(WIN_B, T):, 4:]


def nsa_decode(xn, pool, page_table, layer, buf, w_in, w_out, w_cmp1, w_cmp2, pe_cmp, rel_bias):
    DB, S, _ = xn.shape
    q, kv, g = nsa_project(xn, w_in)
    past = pool[page_table, :, layer, 0:2].reshape(DB, PAST_LEN, 2, N_KV, HEAD_DIM)
    kc = jnp.concatenate([past[:, :, 0], kv[:, :, 0]], axis=1)
    vc = jnp.concatenate([past[:, :, 1], kv[:, :, 1]], axis=1)
    qpos = PAST_LEN + jnp.arange(S)
    o_c, p = compressed_branch(q, kc, vc, w_cmp1, w_cmp2, pe_cmp, qpos, rel_bias)
    idx, valid = select_blocks(p, qpos, PAST_LEN // SEL_BLOCK + -(-S // SEL_BLOCK))
    o_s = selected_decode(q, kv[:, :, 2], kv[:, :, 3], pool, page_table, layer, idx, valid, qpos, rel_bias)
    o_w, new_buf = window_decode(q, kv[:, :, 4], kv[:, :, 5], buf, WIN_B, rel_bias)
    y = nsa_merge(g, o_c, o_s, o_w, w_out)
    return y, kv[:, :, :4], new_buf


def setup_inputs(seed: int = 0) -> dict:
    key = jax.random.key(seed)
    ks = jax.random.split(key, 24)
    f32 = jnp.float32
    nrm = lambda k, shape, s=1.0: (jax.random.normal(k, shape, f32) * s)
    n_pages = PAST_LEN // PAGE_SIZE
    used = DEC_BATCH * n_pages
    n_pool = used + (used + 3) // 4
    lba = min(WIN_A, PAST_LEN)
    lbb = min(WIN_B, PAST_LEN)
    page_table = jax.random.permutation(ks[5], n_pool)[:used].reshape(DEC_BATCH, n_pages).astype(jnp.int32)
    return {
        'x_prompt': nrm(ks[0], (BATCH, SEQ, D_MODEL)),
        'x_sample': nrm(ks[1], (DEC_BATCH, DEC_SEQ, D_MODEL)),
        'state_swa': nrm(ks[2], (DEC_BATCH, lba, N_LAYERS_A, 2, N_KV, HEAD_DIM)),
        'cache_nsa': nrm(ks[3], (n_pool, PAGE_SIZE, N_LAYERS_B, 4, N_KV, HEAD_DIM)),
        'state_nsa_win': nrm(ks[4], (DEC_BATCH, lbb, N_LAYERS_B, 2, N_KV, HEAD_DIM)),
        'page_table': page_table,
        'rel_bias': nrm(ks[6], (N_BUCKETS, N_HEADS), 0.5),
        'norm_mix': 1.0 + nrm(ks[7], (DEPTH, D_MODEL), 0.02),
        'norm_mlp': 1.0 + nrm(ks[8], (DEPTH, D_MODEL), 0.02),
        'norm_final': 1.0 + nrm(ks[9], (D_MODEL,), 0.02),
        'w_in_a': nrm(ks[10], (N_LAYERS_A, D_MODEL, IN_A), D_MODEL ** -0.5),
        'b_in_a': nrm(ks[11], (N_LAYERS_A, IN_A), 0.02),
        'w_out_a': nrm(ks[12], (N_LAYERS_A, HD, D_MODEL), HD ** -0.5),
        'b_out_a': nrm(ks[13], (N_LAYERS_A, D_MODEL), 0.02),
        'sinks_a': nrm(ks[14], (N_LAYERS_A, N_HEADS)),
        'w_in_b': nrm(ks[15], (N_LAYERS_B, D_MODEL, IN_B), D_MODEL ** -0.5),
        'w_out_b': nrm(ks[16], (N_LAYERS_B, HD, D_MODEL), HD ** -0.5),
        'w_cmp1': nrm(ks[17], (N_LAYERS_B, 2, CMP_LEN, HEAD_DIM, CMP_HIDDEN), (CMP_LEN * HEAD_DIM) ** -0.5),
        'w_cmp2': nrm(ks[18], (N_LAYERS_B, 2, CMP_HIDDEN, HEAD_DIM), CMP_HIDDEN ** -0.5),
        'pe_cmp': nrm(ks[19], (N_LAYERS_B, 2, CMP_LEN, HEAD_DIM), 0.1),
        'w_up': nrm(ks[20], (DEPTH, D_MODEL, D_FF), D_MODEL ** -0.5),
        'w_down': nrm(ks[21], (DEPTH, D_FF, D_MODEL), D_FF ** -0.5),
    }


def reference(x_prompt, x_sample, state_swa, cache_nsa, state_nsa_win, page_table, rel_bias,
              norm_mix, norm_mlp, norm_final, w_in_a, b_in_a, w_out_a, b_out_a, sinks_a,
              w_in_b, w_out_b, w_cmp1, w_cmp2, pe_cmp, w_up, w_down):
    xp, xs = x_prompt, x_sample
    swa_p, swa_s, rows_p, rows_s, win_p, win_s = [], [], [], [], [], []
    for layer in range(DEPTH):
        hp = rmsnorm(xp, norm_mix[layer])
        hs = rmsnorm(xs, norm_mix[layer])
        if layer % 2 == 0:
            a = layer // 2
            yp, st_p = swa_prompt(hp, w_in_a[a], b_in_a[a], w_out_a[a], b_out_a[a], sinks_a[a], rel_bias)
            ys, st_s = swa_decode(hs, state_swa[:, :, a], w_in_a[a], b_in_a[a], w_out_a[a], b_out_a[a], sinks_a[a], rel_bias)
            swa_p.append(st_p)
            swa_s.append(st_s)
        else:
            b = layer // 2
            yp, r_p, w_p = nsa_prompt(hp, w_in_b[b], w_out_b[b], w_cmp1[b], w_cmp2[b], pe_cmp[b], rel_bias)
            ys, r_s, w_s = nsa_decode(hs, cache_nsa, page_table, b, state_nsa_win[:, :, b], w_in_b[b], w_out_b[b],
                                      w_cmp1[b], w_cmp2[b], pe_cmp[b], rel_bias)
            rows_p.append(r_p)
            rows_s.append(r_s)
            win_p.append(w_p)
            win_s.append(w_s)
        xp = xp + yp
        xs = xs + ys
        xp = xp + sqrelu_mlp(rmsnorm(xp, norm_mlp[layer]), w_up[layer], w_down[layer])
        xs = xs + sqrelu_mlp(rmsnorm(xs, norm_mlp[layer]), w_up[layer], w_down[layer])
    y_prompt = rmsnorm(xp, norm_final)
    y_sample = rmsnorm(xs, norm_final)
    return (y_prompt, y_sample, jnp.stack(swa_p, axis=2), jnp.stack(swa_s, axis=2),
            jnp.stack(rows_p, axis=2), jnp.stack(rows_s, axis=2),
            jnp.stack(win_p, axis=2), jnp.stack(win_s, axis=2))
```

```python
import functools
import math

import numpy as np
import jax
import jax.numpy as jnp
from jax import lax
from jax.experimental import pallas as pl
from jax.experimental.pallas import tpu as pltpu

D_MODEL = 1024
N_HEADS = 16
HEAD_DIM = 64
N_KV = 2
GROUP = N_HEADS // N_KV
HD = N_HEADS * HEAD_DIM
KVD = N_KV * HEAD_DIM
D_FF = 4 * D_MODEL
PAGE_SIZE = 128
WIN_A = 128
WIN_B = 512
Q_BLOCK = 128
CMP_STRIDE = 16
CMP_LEN = 2 * CMP_STRIDE
CMP_HIDDEN = 128
SEL_BLOCK = 64
N_SELECT = 16
FORCED_SCORE = 1e9
N_BUCKETS = 32
MAX_DISTANCE = 128
NORM_EPS = 1e-5
SCALE = HEAD_DIM ** -0.5

NEG = -1e30
REMOVED = -2e30
F32 = jnp.float32
BF16 = jnp.bfloat16
VMEM_LIMIT = 56 * 1024 * 1024


def _cparams(*sem):
    return pltpu.CompilerParams(dimension_semantics=sem, vmem_limit_bytes=VMEM_LIMIT)


def _const_spec(shape):
    nd = len(shape)
    return pl.BlockSpec(shape, lambda *_: (0,) * nd, pipeline_mode=pl.Buffered(1))


def _bucket_np(dist):
    n = np.maximum(dist, 0)
    exact = N_BUCKETS // 2
    nf = np.maximum(n, exact).astype(np.float32)
    large = exact + (np.log(nf / np.float32(exact)) / np.float32(math.log(MAX_DISTANCE / exact))
                     * np.float32(N_BUCKETS - exact)).astype(np.int32)
    return np.where(n < exact, n, np.minimum(large, N_BUCKETS - 1)).astype(np.int32)


def _bias_tile(rel_bias, dist, valid, shift):
    tb = rel_bias.astype(F32)
    if shift:
        tb = tb - tb[N_BUCKETS - 1][None, :]
    b = tb[jnp.asarray(_bucket_np(dist))]
    b = jnp.where(jnp.asarray(valid)[..., None], b, NEG)
    return jnp.moveaxis(b, -1, 0)


def _rms(x, g):
    ms = jnp.mean(x * x, axis=-1, keepdims=True)
    return x * lax.rsqrt(ms + NORM_EPS) * g


def _dot(a, b):
    return jnp.dot(a, b, preferred_element_type=F32)


def _dot_nt(a, b):
    return lax.dot_general(a, b, (((1,), (1,)), ((), ())), preferred_element_type=F32)


def _dot_f32_by_01(a, b01):
    hi = a.astype(BF16)
    r1 = a - hi.astype(F32)
    mid = r1.astype(BF16)
    lo = (r1 - mid.astype(F32)).astype(BF16)
    return _dot(hi, b01) + _dot(mid, b01) + _dot(lo, b01)


def _topk_mask(score, k):
    ncols = score.shape[1]
    iota = lax.broadcasted_iota(jnp.int32, score.shape, 1).astype(F32)
    sel = jnp.zeros_like(score)
    s = score
    for _ in range(k):
        m = jnp.max(s, axis=1, keepdims=True)
        first = jnp.min(jnp.where(s == m, iota, float(ncols)), axis=1, keepdims=True)
        hit = iota == first
        ok = jnp.where(m > 0.5 * NEG, 1.0, 0.0)
        sel = jnp.where(hit, ok, sel)
        s = jnp.where(hit, REMOVED, s)
    return sel


def _slabs_add(s, tiles):
    q = s.shape[0] // GROUP
    return jnp.concatenate([s[r * q:(r + 1) * q] + tiles[r] for r in range(GROUP)], axis=0)


def _proj_kernel(*refs, n_kv, has_bias, has_gate):
    it = iter(refs)
    x_ref, g_ref, w_ref = next(it), next(it), next(it)
    b_ref = next(it) if has_bias else None
    q_ref, kv_ref, kvb_ref = next(it), next(it), next(it)
    gl_ref = next(it) if has_gate else None
    xn = _rms(x_ref[...], g_ref[...]).astype(BF16)
    u = _dot(xn, w_ref[...])
    if has_bias:
        u = u + b_ref[...]
    for h in range(N_HEADS):
        q_ref[h] = (u[:, h * HEAD_DIM:(h + 1) * HEAD_DIM] * SCALE).astype(BF16)
    kv = u[:, HD:HD + n_kv]
    kv_ref[...] = kv
    kvb_ref[...] = kv.astype(BF16)
    if has_gate:
        gl_ref[...] = u[:, HD + n_kv:]


def _project(x, g, w, b, n_kv):
    n = x.shape[0]
    dout = w.shape[1]
    n_gate = dout - HD - n_kv
    tm = min(512, n)
    assert n % tm == 0
    has_bias = b is not None
    has_gate = n_gate > 0
    ins = [x, g.reshape(1, D_MODEL), w.astype(BF16)]
    in_specs = [pl.BlockSpec((tm, D_MODEL), lambda t: (t, 0)),
                _const_spec((1, D_MODEL)), _const_spec((D_MODEL, dout))]
    if has_bias:
        ins.append(b.reshape(1, dout))
        in_specs.append(_const_spec((1, dout)))
    out_shape = [jax.ShapeDtypeStruct((N_HEADS, n, HEAD_DIM), BF16),
                 jax.ShapeDtypeStruct((n, n_kv), F32),
                 jax.ShapeDtypeStruct((n, n_kv), BF16)]
    out_specs = [pl.BlockSpec((N_HEADS, tm, HEAD_DIM), lambda t: (0, t, 0)),
                 pl.BlockSpec((tm, n_kv), lambda t: (t, 0)),
                 pl.BlockSpec((tm, n_kv), lambda t: (t, 0))]
    if has_gate:
        out_shape.append(jax.ShapeDtypeStruct((n, n_gate), F32))
        out_specs.append(pl.BlockSpec((tm, n_gate), lambda t: (t, 0)))
    outs = pl.pallas_call(
        functools.partial(_proj_kernel, n_kv=n_kv, has_bias=has_bias, has_gate=has_gate),
        grid=(n // tm,), in_specs=in_specs, out_specs=out_specs, out_shape=out_shape,
        compiler_params=_cparams("parallel"), name="norm_proj")(*ins)
    return outs if has_gate else (*outs, None)


FF_CHUNK = 1024


def _out_mlp_kernel(*refs, n_o, has_bias, final):
    it = iter(refs)
    x_ref = next(it)
    o_refs = [next(it) for _ in range(n_o)]
    wo_ref = next(it)
    bo_ref = next(it) if has_bias else None
    gm_ref, wup_ref, wdn_ref = next(it), next(it), next(it)
    gf_ref = next(it) if final else None
    y_ref = next(it)
    o = o_refs[0][...]
    for r in o_refs[1:]:
        o = o + r[...]
    x1 = x_ref[...] + _dot(o.astype(BF16), wo_ref[...])
    if has_bias:
        x1 = x1 + bo_ref[...]
    xn = _rms(x1, gm_ref[...]).astype(BF16)
    acc = x1
    for c in range(D_FF // FF_CHUNK):
        h = jnp.maximum(_dot(xn, wup_ref[:, c * FF_CHUNK:(c + 1) * FF_CHUNK]), 0.0)
        acc = acc + _dot((h * h).astype(BF16), wdn_ref[c * FF_CHUNK:(c + 1) * FF_CHUNK, :])
    y_ref[...] = _rms(acc, gf_ref[...]) if final else acc


def _out_mlp(x, os_, w_out, b_out, g_mlp, w_up, w_down, g_final):
    n = x.shape[0]
    tm = min(256, n)
    assert n % tm == 0
    has_bias = b_out is not None
    final = g_final is not None
    row = pl.BlockSpec((tm, D_MODEL), lambda t: (t, 0))
    ins = [x, *os_, w_out.astype(BF16)]
    in_specs = [row] + [row] * len(os_) + [_const_spec((HD, D_MODEL))]
    if has_bias:
        ins.append(b_out.reshape(1, D_MODEL))
        in_specs.append(_const_spec((1, D_MODEL)))
    ins += [g_mlp.reshape(1, D_MODEL), w_up.astype(BF16), w_down.astype(BF16)]
    in_specs += [_const_spec((1, D_MODEL)), _const_spec((D_MODEL, D_FF)), _const_spec((D_FF, D_MODEL))]
    if final:
        ins.append(g_final.reshape(1, D_MODEL))
        in_specs.append(_const_spec((1, D_MODEL)))
    return pl.pallas_call(
        functools.partial(_out_mlp_kernel, n_o=len(os_), has_bias=has_bias, final=final),
        grid=(n // tm,), in_specs=in_specs, out_specs=row,
        out_shape=jax.ShapeDtypeStruct((n, D_MODEL), F32),
        compiler_params=_cparams("parallel"), name="out_mlp")(*ins)


def _banded_kernel(*refs, kinds, use_sink, gate_off):
    it = iter(refs)
    q_ref, k_ref, v_ref, bt_ref, mt_ref = next(it), next(it), next(it), next(it), next(it)
    sink_ref = next(it) if use_sink else None
    gl_ref = next(it) if gate_off is not None else None
    o_ref = next(it)
    i = pl.program_id(1)
    gates = jax.nn.sigmoid(gl_ref[...]) if gate_off is not None else None
    bias_slot = {}
    for j, kind in enumerate(kinds):
        if kind == "bias":
            bias_slot[j] = len(bias_slot)
    for g in range(N_KV):
        qs = q_ref[g * GROUP:(g + 1) * GROUP].reshape(GROUP * Q_BLOCK, HEAD_DIM)
        cols = slice(g * HEAD_DIM, (g + 1) * HEAD_DIM)
        s_parts, v_parts = [], []
        for j, kind in enumerate(kinds):
            start = pl.multiple_of(jnp.maximum(i - j, 0) * Q_BLOCK, Q_BLOCK)
            kj = k_ref[pl.ds(start, Q_BLOCK), cols]
            v_parts.append(v_ref[pl.ds(start, Q_BLOCK), cols])
            sj = _dot_nt(qs, kj)
            pen = jnp.where(i >= j, 0.0, NEG) if j > 0 else 0.0
            if kind == "bias":
                sj = _slabs_add(sj, [bt_ref[bias_slot[j], g * GROUP + r] + pen for r in range(GROUP)])
            elif kind == "mask":
                mt = mt_ref[...] + pen
                sj = _slabs_add(sj, [mt] * GROUP)
            else:
                sj = sj + pen
            s_parts.append(sj)
        s = jnp.concatenate(s_parts, axis=1)
        v = jnp.concatenate(v_parts, axis=0)
        m = jnp.max(s, axis=1, keepdims=True)
        if use_sink:
            sk = jnp.concatenate([jnp.full((Q_BLOCK, 1), sink_ref[g * GROUP + r], F32) for r in range(GROUP)], axis=0)
            m = jnp.maximum(m, sk)
        e = jnp.exp(s - m)
        den = jnp.sum(e, axis=1, keepdims=True)
        if use_sink:
            den = den + jnp.exp(sk - m)
        acc = _dot(e.astype(BF16), v)
        for r in range(GROUP):
            h = g * GROUP + r
            f = 1.0 / den[r * Q_BLOCK:(r + 1) * Q_BLOCK]
            if gates is not None:
                f = f * gates[:, gate_off + h:gate_off + h + 1]
            o_ref[:, h * HEAD_DIM:(h + 1) * HEAD_DIM] = acc[r * Q_BLOCK:(r + 1) * Q_BLOCK] * f


def _banded(q, kvb, k_col, v_col, n_batch, seq, kinds, bias_tiles, mask_tile, sinks, gate_logits, gate_off):
    nq = seq // Q_BLOCK
    n = n_batch * seq
    use_sink = sinks is not None
    ins = [q, kvb, kvb, bias_tiles, mask_tile]
    in_specs = [pl.BlockSpec((N_HEADS, Q_BLOCK, HEAD_DIM), lambda b, i: (0, b * nq + i, 0)),
                pl.BlockSpec((seq, 2 * HEAD_DIM), lambda b, i: (b, k_col)),
                pl.BlockSpec((seq, 2 * HEAD_DIM), lambda b, i: (b, v_col)),
                _const_spec(bias_tiles.shape), _const_spec(mask_tile.shape)]
    if use_sink:
        ins.append(sinks.astype(F32))
        in_specs.append(pl.BlockSpec(memory_space=pltpu.SMEM))
    if gate_logits is not None:
        ins.append(gate_logits)
        in_specs.append(pl.BlockSpec((Q_BLOCK, gate_logits.shape[1]), lambda b, i: (b * nq + i, 0)))
    else:
        gate_off = None
    return pl.pallas_call(
        functools.partial(_banded_kernel, kinds=kinds, use_sink=use_sink, gate_off=gate_off),
        grid=(n_batch, nq), in_specs=in_specs,
        out_specs=pl.BlockSpec((Q_BLOCK, HD), lambda b, i: (b * nq + i, 0)),
        out_shape=jax.ShapeDtypeStruct((n, HD), F32),
        compiler_params=_cparams("parallel", "parallel"), name="banded_attn")(*ins)


def _compress(load_xp, wc_ref, w1f_ref, pef_ref, w2_ref, nch):
    outs = []
    for t in range(2):
        acc = None
        for pp in range(CMP_STRIDE // 2):
            x2 = jnp.concatenate([load_xp(t, 2 * pp), load_xp(t, 2 * pp + 1)], axis=1).astype(BF16)
            d = _dot(x2, wc_ref[t, pp])
            acc = d if acc is None else acc + d
        pet = _dot(pef_ref[t], w1f_ref[t])
        lo = acc[:, :2 * CMP_HIDDEN]
        hi_next = pltpu.roll(acc[:, 2 * CMP_HIDDEN:], nch - 1, 0)
        hh = lo + hi_next + jnp.concatenate([pet, pet], axis=1)
        a = (hh * jax.nn.sigmoid(hh)).astype(BF16)
        outs.append(_dot(a, w2_ref[t]))
    return outs


def _compress_weights(w1, w2, pe):
    z = jnp.zeros((2, CMP_STRIDE, HEAD_DIM, CMP_HIDDEN), w1.dtype)
    lo, hi = w1[:, :CMP_STRIDE], w1[:, CMP_STRIDE:]
    top = jnp.concatenate([lo, z, hi, z], axis=-1)
    bot = jnp.concatenate([z, lo, z, hi], axis=-1)
    wp = jnp.concatenate([top, bot], axis=2)
    wc = wp.reshape(2, CMP_STRIDE // 2, 2 * 2 * HEAD_DIM, 4 * CMP_HIDDEN).astype(BF16)
    z2 = jnp.zeros((2, CMP_HIDDEN, HEAD_DIM), w2.dtype)
    w2bd = jnp.concatenate([jnp.concatenate([w2, z2], axis=-1), jnp.concatenate([z2, w2], axis=-1)], axis=1)
    w1f = w1.reshape(2, CMP_LEN * HEAD_DIM, CMP_HIDDEN).astype(BF16)
    pef = pe.reshape(2, 1, CMP_LEN * HEAD_DIM).astype(BF16)
    return wc, w1f, pef, w2bd.astype(BF16)


def _compress_prompt_kernel(x_ref, wc_ref, w1f_ref, pef_ref, w2_ref, kc_ref, vc_ref, *, nch):
    def load_xp(t, p):
        return x_ref[:, p, t * 2 * HEAD_DIM:(t + 1) * 2 * HEAD_DIM]
    kc, vc = _compress(load_xp, wc_ref, w1f_ref, pef_ref, w2_ref, nch)
    kc_ref[...] = kc
    vc_ref[...] = vc


def _compress_prompt(kvf, n_batch, seq, cw):
    nch = seq // CMP_STRIDE
    wc, w1f, pef, w2bd = cw
    out = jax.ShapeDtypeStruct((n_batch * nch, 2 * HEAD_DIM), F32)
    ospec = pl.BlockSpec((nch, 2 * HEAD_DIM), lambda b: (b, 0))
    return pl.pallas_call(
        functools.partial(_compress_prompt_kernel, nch=nch),
        grid=(n_batch,),
        in_specs=[pl.BlockSpec((nch, CMP_STRIDE, 4 * HEAD_DIM), lambda b: (b, 0, 0)),
                  _const_spec(wc.shape), _const_spec(w1f.shape), _const_spec(pef.shape), _const_spec(w2bd.shape)],
        out_specs=[ospec, ospec], out_shape=[out, out],
        compiler_params=_cparams("parallel"), name="compress_prompt")(
            kvf.reshape(n_batch * nch, CMP_STRIDE, kvf.shape[1]), wc, w1f, pef, w2bd)


NEAR_CMP = 16


def _cmp_select_kernel(q_ref, kc_ref, vc_ref, nt_ref, ovl_ref, gl_ref, o_ref, sel_ref, *, nch, n_blocks):
    i = pl.program_id(1)
    gates = jax.nn.sigmoid(gl_ref[...])
    cs = i * (Q_BLOCK // CMP_STRIDE) - NEAR_CMP // 2
    start = pl.multiple_of(jnp.maximum(cs, 0), 8)
    variant = jnp.minimum(i, 1)
    col = lax.broadcasted_iota(jnp.int32, (Q_BLOCK, nch), 1)
    far_pen = jnp.where(col < cs, 0.0, NEG)
    kc_all = kc_ref[...].astype(BF16)
    vc_all = vc_ref[...].astype(BF16)
    kc_near = kc_ref[pl.ds(start, NEAR_CMP), :].astype(BF16)
    vc_near = vc_ref[pl.ds(start, NEAR_CMP), :].astype(BF16)
    ovl_near = ovl_ref[pl.ds(start, NEAR_CMP), :]
    r_io = lax.broadcasted_iota(jnp.int32, (Q_BLOCK, n_blocks), 0)
    j_io = lax.broadcasted_iota(jnp.int32, (Q_BLOCK, n_blocks), 1)
    cur = (i * Q_BLOCK + r_io) // SEL_BLOCK
    forced = (j_io == 0) | (j_io == cur) | (j_io == cur - 1)
    sels = []
    for g in range(N_KV):
        qs = q_ref[g * GROUP:(g + 1) * GROUP].reshape(GROUP * Q_BLOCK, HEAD_DIM)
        cols = slice(g * HEAD_DIM, (g + 1) * HEAD_DIM)
        s_far = _slabs_add(_dot_nt(qs, kc_all[:, cols]), [far_pen] * GROUP)
        s_near = _slabs_add(_dot_nt(qs, kc_near[:, cols]),
                            [nt_ref[variant, g * GROUP + r] for r in range(GROUP)])
        m = jnp.maximum(jnp.max(s_far, axis=1, keepdims=True), jnp.max(s_near, axis=1, keepdims=True))
        m = jnp.where(m > 0.5 * NEG, m, 0.0)
        e_far = jnp.exp(s_far - m)
        e_near = jnp.exp(s_near - m)
        den = jnp.sum(e_far, axis=1, keepdims=True) + jnp.sum(e_near, axis=1, keepdims=True)
        inv = 1.0 / jnp.where(den > 0, den, 1.0)
        acc = _dot(e_far.astype(BF16), vc_all[:, cols]) + _dot(e_near.astype(BF16), vc_near[:, cols])
        p_far = e_far * inv
        p_near = e_near * inv
        pg_far = p_far[0:Q_BLOCK]
        pg_near = p_near[0:Q_BLOCK]
        for r in range(GROUP):
            h = g * GROUP + r
            rows = slice(r * Q_BLOCK, (r + 1) * Q_BLOCK)
            if r > 0:
                pg_far = pg_far + p_far[rows]
                pg_near = pg_near + p_near[rows]
            o_ref[:, h * HEAD_DIM:(h + 1) * HEAD_DIM] = acc[rows] * (inv[rows] * gates[:, h:h + 1])
        imp = _dot_f32_by_01(pg_far, ovl_ref[...]) + _dot_f32_by_01(pg_near, ovl_near)
        score = jnp.where(j_io <= cur, jnp.where(forced, FORCED_SCORE, imp), NEG)
        sels.append(_topk_mask(score, min(N_SELECT, n_blocks)))
    sel_ref[...] = jnp.concatenate(sels, axis=1).astype(BF16)


def _overlap_np(nch, n_blocks, n_cols):
    ci = np.arange(nch)[:, None] * CMP_STRIDE
    sj = np.arange(n_cols)[None, :] * SEL_BLOCK
    ov = (ci < sj + SEL_BLOCK) & (ci + CMP_LEN > sj) & (np.arange(n_cols)[None, :] < n_blocks)
    ov[nch - 1] = False
    return ov.astype(np.float32)


def _cmp_select(q, kc, vc, near_tiles, gate_logits, n_batch, seq):
    nq = seq // Q_BLOCK
    nch = seq // CMP_STRIDE
    n_blocks = seq // SEL_BLOCK
    n = n_batch * seq
    ovl = jnp.asarray(_overlap_np(nch, n_blocks, n_blocks), BF16)
    cspec = pl.BlockSpec((nch, 2 * HEAD_DIM), lambda b, i: (b, 0))
    rows = lambda w: pl.BlockSpec((Q_BLOCK, w), lambda b, i: (b * nq + i, 0))
    return pl.pallas_call(
        functools.partial(_cmp_select_kernel, nch=nch, n_blocks=n_blocks),
        grid=(n_batch, nq),
        in_specs=[pl.BlockSpec((N_HEADS, Q_BLOCK, HEAD_DIM), lambda b, i: (0, b * nq + i, 0)),
                  cspec, cspec, _const_spec(near_tiles.shape), _const_spec(ovl.shape),
                  rows(gate_logits.shape[1])],
        out_specs=[rows(HD), rows(N_KV * n_blocks)],
        out_shape=[jax.ShapeDtypeStruct((n, HD), F32), jax.ShapeDtypeStruct((n, N_KV * n_blocks), BF16)],
        compiler_params=_cparams("parallel", "parallel"), name="cmp_select")(
            q, kc, vc, near_tiles, ovl, gate_logits)


FAR_CHUNK = 512


def _selected_kernel(q_ref, k_ref, v_ref, sel_ref, ex_ref, bt_ref, gl_ref, o_ref,
                     mall_ref, m_ref, l_ref, acc_ref, *, n_blocks):
    i = pl.program_id(1)
    gates = jax.nn.sigmoid(gl_ref[...])
    far_end = (i - 1) * Q_BLOCK
    n_far = (i + 2) // (FAR_CHUNK // Q_BLOCK)
    for g in range(N_KV):
        qs = q_ref[g * GROUP:(g + 1) * GROUP].reshape(GROUP * Q_BLOCK, HEAD_DIM)
        cols = slice(g * HEAD_DIM, (g + 1) * HEAD_DIM)
        mall_ref[...] = _dot(sel_ref[:, g * n_blocks:(g + 1) * n_blocks], ex_ref[...])
        s_parts, v_parts = [], []
        for j in (1, 0):
            start = pl.multiple_of(jnp.maximum(i - j, 0) * Q_BLOCK, Q_BLOCK)
            v_parts.append(v_ref[pl.ds(start, Q_BLOCK), cols])
            mb = (mall_ref[:, pl.ds(start, Q_BLOCK)] - 1.0) * (-NEG)
            if j > 0:
                mb = mb + jnp.where(i >= j, 0.0, NEG)
            sj = _dot_nt(qs, k_ref[pl.ds(start, Q_BLOCK), cols])
            s_parts.append(_slabs_add(sj, [bt_ref[j, g * GROUP + r] + mb for r in range(GROUP)]))
        s = jnp.concatenate(s_parts, axis=1)
        m0 = jnp.max(s, axis=1, keepdims=True)
        e = jnp.exp(s - m0)
        m_ref[...] = m0
        l_ref[...] = jnp.sum(e, axis=1, keepdims=True)
        acc_ref[...] = _dot(e.astype(BF16), jnp.concatenate(v_parts, axis=0))

        def far_step(c, carry):
            start = pl.multiple_of(c * FAR_CHUNK, FAR_CHUNK)
            kc = k_ref[pl.ds(start, FAR_CHUNK), cols]
            vc = v_ref[pl.ds(start, FAR_CHUNK), cols]
            mb = (mall_ref[:, pl.ds(start, FAR_CHUNK)] - 1.0) * (-NEG)
            kpos = start + lax.broadcasted_iota(jnp.int32, (Q_BLOCK, FAR_CHUNK), 1)
            mb = jnp.where(kpos < far_end, mb, NEG)
            sc = _slabs_add(_dot_nt(qs, kc), [mb] * GROUP)
            m_old = m_ref[...]
            m_new = jnp.maximum(m_old, jnp.max(sc, axis=1, keepdims=True))
            alpha = jnp.exp(m_old - m_new)
            p = jnp.exp(sc - m_new)
            l_ref[...] = alpha * l_ref[...] + jnp.sum(p, axis=1, keepdims=True)
            acc_ref[...] = alpha * acc_ref[...] + _dot(p.astype(BF16), vc)
            m_ref[...] = m_new
            return carry

        lax.fori_loop(0, n_far, far_step, 0)
        acc = acc_ref[...]
        den = l_ref[...]
        for r in range(GROUP):
            h = g * GROUP + r
            rows = slice(r * Q_BLOCK, (r + 1) * Q_BLOCK)
            f = gates[:, N_HEADS + h:N_HEADS + h + 1] / den[rows]
            o_ref[:, h * HEAD_DIM:(h + 1) * HEAD_DIM] = acc[rows] * f


def _selected(q, kvb, sel, near_tiles, gate_logits, n_batch, seq):
    assert seq % FAR_CHUNK == 0
    nq = seq // Q_BLOCK
    n_blocks = seq // SEL_BLOCK
    n = n_batch * seq
    ex = (np.arange(n_blocks)[:, None] == (np.arange(seq)[None, :] // SEL_BLOCK)).astype(np.float32)
    ex = jnp.asarray(ex, BF16)
    rows = lambda w: pl.BlockSpec((Q_BLOCK, w), lambda b, i: (b * nq + i, 0))
    gq = GROUP * Q_BLOCK
    return pl.pallas_call(
        functools.partial(_selected_kernel, n_blocks=n_blocks),
        grid=(n_batch, nq),
        in_specs=[pl.BlockSpec((N_HEADS, Q_BLOCK, HEAD_DIM), lambda b, i: (0, b * nq + i, 0)),
                  pl.BlockSpec((seq, 2 * HEAD_DIM), lambda b, i: (b, 2)),
                  pl.BlockSpec((seq, 2 * HEAD_DIM), lambda b, i: (b, 3)),
                  rows(N_KV * n_blocks), _const_spec(ex.shape), _const_spec(near_tiles.shape),
                  rows(gate_logits.shape[1])],
        out_specs=rows(HD),
        out_shape=jax.ShapeDtypeStruct((n, HD), F32),
        scratch_shapes=[pltpu.VMEM((Q_BLOCK, seq), F32), pltpu.VMEM((gq, 1), F32),
                        pltpu.VMEM((gq, 1), F32), pltpu.VMEM((gq, HEAD_DIM), F32)],
        compiler_params=_cparams("parallel", "parallel"), name="selected_attn")(
            q, kvb, kvb, sel, ex, near_tiles, gate_logits)


def _block_diag_q(q):
    z = jnp.zeros_like(q)
    row = lax.broadcasted_iota(jnp.int32, (N_HEADS, 2 * HEAD_DIM), 0)
    return jnp.where(row < GROUP, jnp.concatenate([q, z], axis=1), jnp.concatenate([z, q], axis=1))


def _pick_group_half(o_full):
    row = lax.broadcasted_iota(jnp.int32, (N_HEADS, HEAD_DIM), 0)
    return jnp.where(row < GROUP, o_full[:, :HEAD_DIM], o_full[:, HEAD_DIM:])


def _window_decode_kernel(*refs, n_samp, use_sink, use_gate):
    it = iter(refs)
    q_ref, nkv_ref, st_ref, br_ref, b0_ref = next(it), next(it), next(it), next(it), next(it)
    sink_ref = next(it) if use_sink else None
    gl_ref = next(it) if use_gate else None
    o_ref = next(it)
    for s_i in range(n_samp):
        qbd = _block_diag_q(q_ref[s_i])
        kb = st_ref[s_i, :, 0:2 * HEAD_DIM].astype(BF16)
        vb = st_ref[s_i, :, 2 * HEAD_DIM:4 * HEAD_DIM].astype(BF16)
        new = nkv_ref[s_i]
        s_buf = _dot_nt(qbd, kb) + br_ref[...]
        s_new = jnp.sum(qbd.astype(F32) * new[:, 0:2 * HEAD_DIM], axis=1, keepdims=True) + b0_ref[...]
        m = jnp.maximum(jnp.max(s_buf, axis=1, keepdims=True), s_new)
        if use_sink:
            m = jnp.maximum(m, sink_ref[...])
        e_buf = jnp.exp(s_buf - m)
        e_new = jnp.exp(s_new - m)
        den = jnp.sum(e_buf, axis=1, keepdims=True) + e_new
        if use_sink:
            den = den + jnp.exp(sink_ref[...] - m)
        o_full = _dot(e_buf.astype(BF16), vb) + e_new * new[:, 2 * HEAD_DIM:4 * HEAD_DIM]
        f = 1.0 / den
        if use_gate:
            f = f * jax.nn.sigmoid(gl_ref[s_i])
        o_ref[s_i] = _pick_group_half(o_full) * f


def _window_decode(q, new_kv, state, layer, bias_row, bias0, sinks, gate_logits):
    db, lb = state.shape[0], state.shape[1]
    n_samp = 8 if db % 8 == 0 else 1
    use_sink = sinks is not None
    use_gate = gate_logits is not None
    ins = [q, new_kv.reshape(db, 1, 4 * HEAD_DIM), state, bias_row, bias0]
    in_specs = [pl.BlockSpec((n_samp, N_HEADS, HEAD_DIM), lambda t: (t, 0, 0)),
                pl.BlockSpec((n_samp, 1, 4 * HEAD_DIM), lambda t: (t, 0, 0)),
                pl.BlockSpec((n_samp, lb, 4 * HEAD_DIM), lambda t: (t, 0, layer)),
                _const_spec(bias_row.shape), _const_spec(bias0.shape)]
    if use_sink:
        ins.append(sinks.astype(F32).reshape(N_HEADS, 1))
        in_specs.append(_const_spec((N_HEADS, 1)))
    if use_gate:
        ins.append(gate_logits)
        in_specs.append(pl.BlockSpec((n_samp, N_HEADS, 1), lambda t: (t, 0, 0)))
    return pl.pallas_call(
        functools.partial(_window_decode_kernel, n_samp=n_samp, use_sink=use_sink, use_gate=use_gate),
        grid=(db // n_samp,), in_specs=in_specs,
        out_specs=pl.BlockSpec((n_samp, N_HEADS, HEAD_DIM), lambda t: (t, 0, 0)),
        out_shape=jax.ShapeDtypeStruct((db, N_HEADS, HEAD_DIM), F32),
        compiler_params=_cparams("parallel"), name="window_decode")(*ins)


ROW_W = 8 * HEAD_DIM


def _nsa_decode_kernel(pt_ref, q_ref, nkv_ref, gl_ref, pool_ref, wc_ref, w1f_ref, pef_ref, w2_ref,
                       cb_ref, sb_ref, b0_ref, ovl_ref, ex_ref, oc_ref, os_ref,
                       buf_a, buf_b, sem, *, layer, n_pages, past, n_blocks, nbp):
    step = pl.program_id(0)
    n_steps = pl.num_programs(0)
    nch = past // CMP_STRIDE
    n_past = past // SEL_BLOCK

    cpp = PAGE_SIZE // CMP_STRIDE

    def page_copy(page, j, buf, slot):
        return pltpu.make_async_copy(
            pool_ref.at[page, :, :, pl.ds(layer * ROW_W, ROW_W)],
            buf.at[pl.ds(j * cpp, cpp)], sem.at[slot])

    def start_fetch(sample, buf, slot):
        def body(j, c):
            page_copy(pt_ref[sample * n_pages + j], j, buf, slot).start()
            return c
        lax.fori_loop(0, n_pages, body, 0)

    def wait_fetch(buf, slot):
        def body(j, c):
            page_copy(0, j, buf, slot).wait()
            return c
        lax.fori_loop(0, n_pages, body, 0)

    def compute(buf, s_i):
        qbd = _block_diag_q(q_ref[s_i])
        new = nkv_ref[s_i]
        gates = jax.nn.sigmoid(gl_ref[s_i])

        def load_xp(t, p):
            return buf[:, p, t * 2 * HEAD_DIM:(t + 1) * 2 * HEAD_DIM]
        kc, vc = _compress(load_xp, wc_ref, w1f_ref, pef_ref, w2_ref, nch)
        s = _dot_nt(qbd, kc.astype(BF16)) + cb_ref[...]
        m = jnp.max(s, axis=1, keepdims=True)
        m = jnp.where(m > 0.5 * NEG, m, 0.0)
        e = jnp.exp(s - m)
        den = jnp.sum(e, axis=1, keepdims=True)
        inv = 1.0 / jnp.where(den > 0, den, 1.0)
        oc_full = _dot(e.astype(BF16), vc.astype(BF16))
        oc_ref[s_i] = _pick_group_half(oc_full) * (inv * gates[0])
        p = e * inv
        j_io = lax.broadcasted_iota(jnp.int32, (1, nbp), 1)
        cur = past // SEL_BLOCK
        forced = (j_io == 0) | (j_io == cur) | (j_io == cur - 1)
        sel_rows = []
        for g in range(N_KV):
            pg = jnp.sum(p[g * GROUP:(g + 1) * GROUP], axis=0, keepdims=True)
            imp = _dot_f32_by_01(pg, ovl_ref[...])
            score = jnp.where(j_io <= cur, jnp.where(forced, FORCED_SCORE, imp), NEG)
            sel_rows.append(_topk_mask(score, min(N_SELECT, n_blocks))[:, 0:n_past])
        row = lax.broadcasted_iota(jnp.int32, (N_HEADS, n_past), 0)
        sel16 = jnp.where(row < GROUP, sel_rows[0], sel_rows[1]).astype(BF16)
        selm = _dot(sel16, ex_ref[...])
        ks = buf[:, :, 4 * HEAD_DIM:6 * HEAD_DIM].reshape(past, 2 * HEAD_DIM).astype(BF16)
        vs = buf[:, :, 6 * HEAD_DIM:8 * HEAD_DIM].reshape(past, 2 * HEAD_DIM).astype(BF16)
        s2 = _dot_nt(qbd, ks) + sb_ref[...] + (selm - 1.0) * (-NEG)
        s_new = jnp.sum(qbd.astype(F32) * new[:, 0:2 * HEAD_DIM], axis=1, keepdims=True) + b0_ref[...]
        m2 = jnp.maximum(jnp.max(s2, axis=1, keepdims=True), s_new)
        e2 = jnp.exp(s2 - m2)
        e_new = jnp.exp(s_new - m2)
        den2 = jnp.sum(e2, axis=1, keepdims=True) + e_new
        os_full = _dot(e2.astype(BF16), vs) + e_new * new[:, 2 * HEAD_DIM:4 * HEAD_DIM]
        os_ref[s_i] = _pick_group_half(os_full) * (gates[1] / den2)

    @pl.when(step == 0)
    def _():
        start_fetch(0, buf_a, 0)

    start_fetch(2 * step + 1, buf_b, 1)
    wait_fetch(buf_a, 0)
    compute(buf_a, 0)

    @pl.when(step + 1 < n_steps)
    def _():
        start_fetch(2 * step + 2, buf_a, 0)

    wait_fetch(buf_b, 1)
    compute(buf_b, 1)


def _nsa_decode(q, new_sel, gate_logits2, pool, page_table, layer, cw, cmp_bias, sel_bias, bias0):
    db, n_pages = page_table.shape
    assert db % 2 == 0
    past = n_pages * PAGE_SIZE
    nch = past // CMP_STRIDE
    n_past = past // SEL_BLOCK
    n_blocks = n_past + 1
    nbp = -(-n_blocks // 128) * 128
    wc, w1f, pef, w2bd = cw
    ovl = jnp.asarray(_overlap_np(nch, n_blocks, nbp), BF16)
    ex = jnp.asarray((np.arange(n_past)[:, None] == (np.arange(past)[None, :] // SEL_BLOCK)).astype(np.float32), BF16)
    consts = [wc, w1f, pef, w2bd, cmp_bias, sel_bias, bias0, ovl, ex]
    per2 = lambda *tail: pl.BlockSpec((2,) + tail, lambda t, pt: (t,) + (0,) * len(tail))
    grid_spec = pltpu.PrefetchScalarGridSpec(
        num_scalar_prefetch=1, grid=(db // 2,),
        in_specs=[per2(N_HEADS, HEAD_DIM), per2(1, 4 * HEAD_DIM), per2(2, N_HEADS, 1),
                  pl.BlockSpec(memory_space=pl.ANY)] + [_const_spec(c.shape) for c in consts],
        out_specs=[per2(N_HEADS, HEAD_DIM), per2(N_HEADS, HEAD_DIM)],
        scratch_shapes=[pltpu.VMEM((nch, CMP_STRIDE, ROW_W), F32), pltpu.VMEM((nch, CMP_STRIDE, ROW_W), F32),
                        pltpu.SemaphoreType.DMA((2,))])
    out = jax.ShapeDtypeStruct((db, N_HEADS, HEAD_DIM), F32)
    return pl.pallas_call(
        functools.partial(_nsa_decode_kernel, layer=layer, n_pages=n_pages, past=past,
                          n_blocks=n_blocks, nbp=nbp),
        grid_spec=grid_spec, out_shape=[out, out],
        compiler_params=_cparams("arbitrary"), name="nsa_decode")(
            page_table.reshape(-1), q, new_sel.reshape(db, 1, 4 * HEAD_DIM), gate_logits2, pool, *consts)


def _shift_kernel(st_ref, new_ref, o_ref, sem, *, keep, w):
    body = pltpu.make_async_copy(st_ref.at[:, pl.ds(w, keep)], o_ref.at[:, pl.ds(0, keep)], sem.at[0])
    last = pltpu.make_async_copy(new_ref, o_ref.at[:, pl.ds(keep, w)], sem.at[1])
    body.start()
    last.start()
    body.wait()
    last.wait()


def _shift_state(state, new_rows):
    db, lb, w = state.shape
    out = pl.pallas_call(
        functools.partial(_shift_kernel, keep=(lb - 1) * w, w=w),
        in_specs=[pl.BlockSpec(memory_space=pl.ANY), pl.BlockSpec(memory_space=pl.ANY)],
        out_specs=pl.BlockSpec(memory_space=pl.ANY),
        out_shape=jax.ShapeDtypeStruct((db, lb * w), state.dtype),
        scratch_shapes=[pltpu.SemaphoreType.DMA((2,))], name="shift_state")(state.reshape(db, lb * w), new_rows)
    return out.reshape(db, lb, w)


def _decode_bias_row(rel_bias, past, kpos, window):
    dist = past - kpos
    valid = (dist >= 0) if window is None else ((dist >= 0) & (dist < window))
    return _bias_tile(rel_bias, dist[None, :], valid[None, :], False)[:, 0, :]


def kernel(x_prompt, x_sample, state_swa, cache_nsa, state_nsa_win, page_table, rel_bias, norm_mix, norm_mlp,
           norm_final, w_in_a, b_in_a, w_out_a, b_out_a, sinks_a, w_in_b, w_out_b, w_cmp1, w_cmp2, pe_cmp,
           w_up, w_down):
    n_batch, seq, _ = x_prompt.shape
    db = x_sample.shape[0]
    assert x_sample.shape[1] == 1
    depth = norm_mix.shape[0]
    n_la, n_lb = w_in_a.shape[0], w_in_b.shape[0]
    lba, lbb = state_swa.shape[1], state_nsa_win.shape[1]
    n_pages = page_table.shape[1]
    past = n_pages * PAGE_SIZE
    assert seq >= WIN_B and lba == WIN_A and lbb == WIN_B and past >= WIN_B

    r = np.arange(Q_BLOCK)[:, None]
    c = np.arange(Q_BLOCK)[None, :]
    d0, d1 = r - c, Q_BLOCK + r - c
    swa_tiles = jnp.stack([_bias_tile(rel_bias, d0, (d0 >= 0) & (d0 < WIN_A), False),
                           _bias_tile(rel_bias, d1, (d1 >= 0) & (d1 < WIN_A), False)])
    near_tiles = jnp.stack([_bias_tile(rel_bias, d0, d0 >= 0, True),
                            _bias_tile(rel_bias, d1, d1 >= 0, True)])
    nback_b = WIN_B // Q_BLOCK
    far_mask = jnp.asarray(np.where(c > r, 0.0, NEG), F32)
    no_mask = jnp.zeros((8, 128), F32)
    cc = np.arange(NEAR_CMP)[None, :]
    dc0 = r - CMP_STRIDE * cc - (CMP_LEN - 1)
    dc1 = r - CMP_STRIDE * (cc - NEAR_CMP // 2) - (CMP_LEN - 1)
    cmp_tiles = jnp.stack([_bias_tile(rel_bias, dc0, dc0 >= 0, True), _bias_tile(rel_bias, dc1, dc1 >= 0, True)])
    nch_d = past // CMP_STRIDE
    cmp_row = _decode_bias_row(rel_bias, past, np.arange(nch_d) * CMP_STRIDE + CMP_LEN - 1, None)
    sel_row = _decode_bias_row(rel_bias, past, np.arange(past), None)
    swa_row = _decode_bias_row(rel_bias, past, past - lba + np.arange(lba), WIN_A)
    win_row = _decode_bias_row(rel_bias, past, past - lbb + np.arange(lbb), WIN_B)
    bias0 = rel_bias.astype(F32)[0].reshape(N_HEADS, 1)

    xp = x_prompt.reshape(n_batch * seq, D_MODEL)
    xs = x_sample.reshape(db, D_MODEL)
    st_swa = state_swa.reshape(db, lba, n_la * 4 * HEAD_DIM)
    st_win = state_nsa_win.reshape(db, lbb, n_lb * 4 * HEAD_DIM)
    pool = cache_nsa.reshape(cache_nsa.shape[0], PAGE_SIZE // CMP_STRIDE, CMP_STRIDE, n_lb * ROW_W)
    swa_p, swa_new, rows_p, rows_s, win_p, win_new = [], [], [], [], [], []

    for layer in range(depth):
        final = norm_final if layer == depth - 1 else None
        if layer % 2 == 0:
            a = layer // 2
            q_p, kv_p, kvb_p, _ = _project(xp, norm_mix[layer], w_in_a[a], b_in_a[a], 2 * KVD)
            q_s, kv_s, _, _ = _project(xs, norm_mix[layer], w_in_a[a], b_in_a[a], 2 * KVD)
            o_p = _banded(q_p, kvb_p, 0, 1, n_batch, seq, ("bias", "bias"), swa_tiles, no_mask,
                          sinks_a[a], None, None)
            o_s = _window_decode(jnp.swapaxes(q_s, 0, 1), kv_s, st_swa, a, swa_row, bias0, sinks_a[a], None)
            swa_p.append(kv_p.reshape(n_batch, seq, 2, N_KV, HEAD_DIM)[:, seq - WIN_A:])
            swa_new.append(kv_s)
            xp = _out_mlp(xp, [o_p], w_out_a[a], b_out_a[a], norm_mlp[layer], w_up[layer], w_down[layer], final)
            xs = _out_mlp(xs, [o_s.reshape(db, HD)], w_out_a[a], b_out_a[a], norm_mlp[layer], w_up[layer],
                          w_down[layer], final)
        else:
            b = layer // 2
            cw = _compress_weights(w_cmp1[b], w_cmp2[b], pe_cmp[b])
            q_p, kv_p, kvb_p, gl_p = _project(xp, norm_mix[layer], w_in_b[b], None, 6 * KVD)
            q_s, kv_s, _, gl_s = _project(xs, norm_mix[layer], w_in_b[b], None, 6 * KVD)
            kc, vc = _compress_prompt(kv_p, n_batch, seq, cw)
            o_c, sel = _cmp_select(q_p, kc, vc, cmp_tiles, gl_p, n_batch, seq)
            o_sel = _selected(q_p, kvb_p, sel, near_tiles, gl_p, n_batch, seq)
            kinds = ("bias", "bias") + ("none",) * (nback_b - 2) + ("mask",)
            o_w = _banded(q_p, kvb_p, 4, 5, n_batch, seq, kinds, near_tiles, far_mask, None, gl_p, 2 * N_HEADS)
            rows_p.append(kv_p[:, :4 * KVD].reshape(n_batch, seq, 4, N_KV, HEAD_DIM))
            win_p.append(kv_p.reshape(n_batch, seq, 6 * KVD)[:, seq - WIN_B:, 4 * KVD:]
                         .reshape(n_batch, WIN_B, 2, N_KV, HEAD_DIM))
            xp = _out_mlp(xp, [o_c, o_sel, o_w], w_out_b[b], None, norm_mlp[layer], w_up[layer], w_down[layer], final)
            q_sd = jnp.swapaxes(q_s, 0, 1)
            gl_s3 = gl_s.reshape(db, 3, N_HEADS, 1)
            oc_s, os_s = _nsa_decode(q_sd, kv_s[:, 2 * KVD:4 * KVD], gl_s3[:, 0:2], pool, page_table, b, cw,
                                     cmp_row, sel_row, bias0)
            ow_s = _window_decode(q_sd, kv_s[:, 4 * KVD:], st_win, b, win_row, bias0, None, gl_s3[:, 2])
            rows_s.append(kv_s[:, :4 * KVD].reshape(db, 1, 4, N_KV, HEAD_DIM))
            win_new.append(kv_s[:, 4 * KVD:])
            xs = _out_mlp(xs, [oc_s.reshape(db, HD), os_s.reshape(db, HD), ow_s.reshape(db, HD)], w_out_b[b], None,
                          norm_mlp[layer], w_up[layer], w_down[layer], final)

    y_prompt = xp.reshape(n_batch, seq, D_MODEL)
    y_sample = xs.reshape(db, 1, D_MODEL)
    swa_sample = _shift_state(st_swa, jnp.concatenate(swa_new, axis=1))
    win_sample = _shift_state(st_win, jnp.concatenate(win_new, axis=1))
    return (y_prompt, y_sample,
            jnp.stack(swa_p, axis=2),
            swa_sample.reshape(db, lba, n_la, 2, N_KV, HEAD_DIM),
            jnp.stack(rows_p, axis=2), jnp.stack(rows_s, axis=2),
            jnp.stack(win_p, axis=2),
            win_sample.reshape(db, lbb, n_lb, 2, N_KV, HEAD_DIM))
```

```python
import functools
import math

import numpy as np
import jax
import jax.numpy as jnp
from jax import lax
from jax.experimental import pallas as pl
from jax.experimental.pallas import tpu as pltpu

D_MODEL = 1024
N_HEADS = 16
HEAD_DIM = 64
N_KV = 2
GROUP = N_HEADS // N_KV
HD = N_HEADS * HEAD_DIM
KVD = N_KV * HEAD_DIM
D_FF = 4 * D_MODEL
PAGE_SIZE = 128
WIN_A = 128
WIN_B = 512
Q_BLOCK = 128
CMP_STRIDE = 16
CMP_LEN = 2 * CMP_STRIDE
CMP_HIDDEN = 128
SEL_BLOCK = 64
N_SELECT = 16
FORCED_SCORE = 1e9
N_BUCKETS = 32
MAX_DISTANCE = 128
NORM_EPS = 1e-5
SCALE = HEAD_DIM ** -0.5

NEG = -1e30
REMOVED = -2e30
F32 = jnp.float32
BF16 = jnp.bfloat16
VMEM_LIMIT = 56 * 1024 * 1024


def _cparams(*sem):
    return pltpu.CompilerParams(dimension_semantics=sem, vmem_limit_bytes=VMEM_LIMIT)


def _const_spec(shape):
    nd = len(shape)
    return pl.BlockSpec(shape, lambda *_: (0,) * nd, pipeline_mode=pl.Buffered(1))


def _bucket_np(dist):
    n = np.maximum(dist, 0)
    exact = N_BUCKETS // 2
    nf = np.maximum(n, exact).astype(np.float32)
    large = exact + (np.log(nf / np.float32(exact)) / np.float32(math.log(MAX_DISTANCE / exact))
                     * np.float32(N_BUCKETS - exact)).astype(np.int32)
    return np.where(n < exact, n, np.minimum(large, N_BUCKETS - 1)).astype(np.int32)


def _bias_tile(rel_bias, dist, valid, shift):
    tb = rel_bias.astype(F32)
    if shift:
        tb = tb - tb[N_BUCKETS - 1][None, :]
    b = tb[jnp.asarray(_bucket_np(dist))]
    b = jnp.where(jnp.asarray(valid)[..., None], b, NEG)
    return jnp.moveaxis(b, -1, 0)


def _rms(x, g):
    ms = jnp.mean(x * x, axis=-1, keepdims=True)
    return x * lax.rsqrt(ms + NORM_EPS) * g


def _dot(a, b):
    return jnp.dot(a, b, preferred_element_type=F32)


def _dot_nt(a, b):
    return lax.dot_general(a, b, (((1,), (1,)), ((), ())), preferred_element_type=F32)


def _dot_f32_by_01(a, b01):
    hi = a.astype(BF16)
    r1 = a - hi.astype(F32)
    mid = r1.astype(BF16)
    lo = (r1 - mid.astype(F32)).astype(BF16)
    return _dot(hi, b01) + _dot(mid, b01) + _dot(lo, b01)


def _topk_mask(score, k):
    ncols = score.shape[1]
    iota = lax.broadcasted_iota(jnp.int32, score.shape, 1).astype(F32)
    sel = jnp.zeros_like(score)
    s = score
    for _ in range(k):
        m = jnp.max(s, axis=1, keepdims=True)
        first = jnp.min(jnp.where(s == m, iota, float(ncols)), axis=1, keepdims=True)
        hit = iota == first
        ok = jnp.where(m > 0.5 * NEG, 1.0, 0.0)
        sel = jnp.where(hit, ok, sel)
        s = jnp.where(hit, REMOVED, s)
    return sel


def _slabs_add(s, tiles):
    q = s.shape[0] // GROUP
    return jnp.concatenate([s[r * q:(r + 1) * q] + tiles[r] for r in range(GROUP)], axis=0)


def _proj_kernel(*refs, n_kv, has_bias, has_gate):
    it = iter(refs)
    x_ref, g_ref, w_ref = next(it), next(it), next(it)
    b_ref = next(it) if has_bias else None
    q_ref, kv_ref, kvb_ref = next(it), next(it), next(it)
    gl_ref = next(it) if has_gate else None
    xn = _rms(x_ref[...], g_ref[...]).astype(BF16)
    u = _dot(xn, w_ref[...])
    if has_bias:
        u = u + b_ref[...]
    for h in range(N_HEADS):
        q_ref[h] = (u[:, h * HEAD_DIM:(h + 1) * HEAD_DIM] * SCALE).astype(BF16)
    kv = u[:, HD:HD + n_kv]
    kv_ref[...] = kv
    kvb_ref[...] = kv.astype(BF16)
    if has_gate:
        gl_ref[...] = u[:, HD + n_kv:]


def _project(x, g, w, b, n_kv):
    n = x.shape[0]
    dout = w.shape[1]
    n_gate = dout - HD - n_kv
    tm = min(512, n)
    assert n % tm == 0
    has_bias = b is not None
    has_gate = n_gate > 0
    ins = [x, g.reshape(1, D_MODEL), w.astype(BF16)]
    in_specs = [pl.BlockSpec((tm, D_MODEL), lambda t: (t, 0)),
                _const_spec((1, D_MODEL)), _const_spec((D_MODEL, dout))]
    if has_bias:
        ins.append(b.reshape(1, dout))
        in_specs.append(_const_spec((1, dout)))
    out_shape = [jax.ShapeDtypeStruct((N_HEADS, n, HEAD_DIM), BF16),
                 jax.ShapeDtypeStruct((n, n_kv), F32),
                 jax.ShapeDtypeStruct((n, n_kv), BF16)]
    out_specs = [pl.BlockSpec((N_HEADS, tm, HEAD_DIM), lambda t: (0, t, 0)),
                 pl.BlockSpec((tm, n_kv), lambda t: (t, 0)),
                 pl.BlockSpec((tm, n_kv), lambda t: (t, 0))]
    if has_gate:
        out_shape.append(jax.ShapeDtypeStruct((n, n_gate), F32))
        out_specs.append(pl.BlockSpec((tm, n_gate), lambda t: (t, 0)))
    outs = pl.pallas_call(
        functools.partial(_proj_kernel, n_kv=n_kv, has_bias=has_bias, has_gate=has_gate),
        grid=(n // tm,), in_specs=in_specs, out_specs=out_specs, out_shape=out_shape,
        compiler_params=_cparams("parallel"), name="norm_proj")(*ins)
    return outs if has_gate else (*outs, None)


FF_CHUNK = 1024


def _out_mlp_kernel(*refs, n_o, has_bias, final):
    it = iter(refs)
    x_ref = next(it)
    o_refs = [next(it) for _ in range(n_o)]
    wo_ref = next(it)
    bo_ref = next(it) if has_bias else None
    gm_ref, wup_ref, wdn_ref = next(it), next(it), next(it)
    gf_ref = next(it) if final else None
    y_ref = next(it)
    o = o_refs[0][...]
    for r in o_refs[1:]:
        o = o + r[...]
    x1 = x_ref[...] + _dot(o.astype(BF16), wo_ref[...])
    if has_bias:
        x1 = x1 + bo_ref[...]
    xn = _rms(x1, gm_ref[...]).astype(BF16)
    acc = x1
    for c in range(D_FF // FF_CHUNK):
        h = jnp.maximum(_dot(xn, wup_ref[:, c * FF_CHUNK:(c + 1) * FF_CHUNK]), 0.0)
        acc = acc + _dot((h * h).astype(BF16), wdn_ref[c * FF_CHUNK:(c + 1) * FF_CHUNK, :])
    y_ref[...] = _rms(acc, gf_ref[...]) if final else acc


def _out_mlp(x, os_, w_out, b_out, g_mlp, w_up, w_down, g_final):
    n = x.shape[0]
    tm = min(256, n)
    assert n % tm == 0
    has_bias = b_out is not None
    final = g_final is not None
    row = pl.BlockSpec((tm, D_MODEL), lambda t: (t, 0))
    ins = [x, *os_, w_out.astype(BF16)]
    in_specs = [row] + [row] * len(os_) + [_const_spec((HD, D_MODEL))]
    if has_bias:
        ins.append(b_out.reshape(1, D_MODEL))
        in_specs.append(_const_spec((1, D_MODEL)))
    ins += [g_mlp.reshape(1, D_MODEL), w_up.astype(BF16), w_down.astype(BF16)]
    in_specs += [_const_spec((1, D_MODEL)), _const_spec((D_MODEL, D_FF)), _const_spec((D_FF, D_MODEL))]
    if final:
        ins.append(g_final.reshape(1, D_MODEL))
        in_specs.append(_const_spec((1, D_MODEL)))
    return pl.pallas_call(
        functools.partial(_out_mlp_kernel, n_o=len(os_), has_bias=has_bias, final=final),
        grid=(n // tm,), in_specs=in_specs, out_specs=row,
        out_shape=jax.ShapeDtypeStruct((n, D_MODEL), F32),
        compiler_params=_cparams("parallel"), name="out_mlp")(*ins)


def _banded_kernel(*refs, kinds, use_sink, gate_off):
    it = iter(refs)
    q_ref, k_ref, v_ref, bt_ref, mt_ref = next(it), next(it), next(it), next(it), next(it)
    sink_ref = next(it) if use_sink else None
    gl_ref = next(it) if gate_off is not None else None
    o_ref = next(it)
    i = pl.program_id(1)
    gates = jax.nn.sigmoid(gl_ref[...]) if gate_off is not None else None
    bias_slot = {}
    for j, kind in enumerate(kinds):
        if kind == "bias":
            bias_slot[j] = len(bias_slot)
    for g in range(N_KV):
        qs = q_ref[g * GROUP:(g + 1) * GROUP].reshape(GROUP * Q_BLOCK, HEAD_DIM)
        cols = slice(g * HEAD_DIM, (g + 1) * HEAD_DIM)
        s_parts, v_parts = [], []
        for j, kind in enumerate(kinds):
            start = pl.multiple_of(jnp.maximum(i - j, 0) * Q_BLOCK, Q_BLOCK)
            kj = k_ref[pl.ds(start, Q_BLOCK), cols]
            v_parts.append(v_ref[pl.ds(start, Q_BLOCK), cols])
            sj = _dot_nt(qs, kj)
            pen = jnp.where(i >= j, 0.0, NEG) if j > 0 else 0.0
            if kind == "bias":
                sj = _slabs_add(sj, [bt_ref[bias_slot[j], g * GROUP + r] + pen for r in range(GROUP)])
            elif kind == "mask":
                mt = mt_ref[...] + pen
                sj = _slabs_add(sj, [mt] * GROUP)
            else:
                sj = sj + pen
            s_parts.append(sj)
        s = jnp.concatenate(s_parts, axis=1)
        v = jnp.concatenate(v_parts, axis=0)
        m = jnp.max(s, axis=1, keepdims=True)
        if use_sink:
            sk = jnp.concatenate([jnp.full((Q_BLOCK, 1), sink_ref[g * GROUP + r], F32) for r in range(GROUP)], axis=0)
            m = jnp.maximum(m, sk)
        e = jnp.exp(s - m)
        den = jnp.sum(e, axis=1, keepdims=True)
        if use_sink:
            den = den + jnp.exp(sk - m)
        acc = _dot(e.astype(BF16), v)
        for r in range(GROUP):
            h = g * GROUP + r
            f = 1.0 / den[r * Q_BLOCK:(r + 1) * Q_BLOCK]
            if gates is not None:
                f = f * gates[:, gate_off + h:gate_off + h + 1]
            o_ref[:, h * HEAD_DIM:(h + 1) * HEAD_DIM] = acc[r * Q_BLOCK:(r + 1) * Q_BLOCK] * f


def _banded(q, kvb, k_col, v_col, n_batch, seq, kinds, bias_tiles, mask_tile, sinks, gate_logits, gate_off):
    nq = seq // Q_BLOCK
    n = n_batch * seq
    use_sink = sinks is not None
    ins = [q, kvb, kvb, bias_tiles, mask_tile]
    in_specs = [pl.BlockSpec((N_HEADS, Q_BLOCK, HEAD_DIM), lambda b, i: (0, b * nq + i, 0)),
                pl.BlockSpec((seq, 2 * HEAD_DIM), lambda b, i: (b, k_col)),
                pl.BlockSpec((seq, 2 * HEAD_DIM), lambda b, i: (b, v_col)),
                _const_spec(bias_tiles.shape), _const_spec(mask_tile.shape)]
    if use_sink:
        ins.append(sinks.astype(F32))
        in_specs.append(pl.BlockSpec(memory_space=pltpu.SMEM))
    if gate_logits is not None:
        ins.append(gate_logits)
        in_specs.append(pl.BlockSpec((Q_BLOCK, gate_logits.shape[1]), lambda b, i: (b * nq + i, 0)))
    else:
        gate_off = None
    return pl.pallas_call(
        functools.partial(_banded_kernel, kinds=kinds, use_sink=use_sink, gate_off=gate_off),
        grid=(n_batch, nq), in_specs=in_specs,
        out_specs=pl.BlockSpec((Q_BLOCK, HD), lambda b, i: (b * nq + i, 0)),
        out_shape=jax.ShapeDtypeStruct((n, HD), F32),
        compiler_params=_cparams("parallel", "parallel"), name="banded_attn")(*ins)


def _compress(load_xp, wc_ref, w1f_ref, pef_ref, w2_ref, nch, prepare=None):
    outs = []
    for t in range(2):
        if prepare is not None:
            prepare(t)
        acc = None
        for pp in range(CMP_STRIDE // 2):
            x2 = jnp.concatenate([load_xp(t, 2 * pp), load_xp(t, 2 * pp + 1)], axis=1).astype(BF16)
            d = _dot(x2, wc_ref[t, pp])
            acc = d if acc is None else acc + d
        pet = _dot(pef_ref[t], w1f_ref[t])
        lo = acc[:, :2 * CMP_HIDDEN]
        hi_next = pltpu.roll(acc[:, 2 * CMP_HIDDEN:], nch - 1, 0)
        hh = lo + hi_next + jnp.concatenate([pet, pet], axis=1)
        a = (hh * jax.nn.sigmoid(hh)).astype(BF16)
        outs.append(_dot(a, w2_ref[t]))
    return outs


def _compress_weights(w1, w2, pe):
    z = jnp.zeros((2, CMP_STRIDE, HEAD_DIM, CMP_HIDDEN), w1.dtype)
    lo, hi = w1[:, :CMP_STRIDE], w1[:, CMP_STRIDE:]
    top = jnp.concatenate([lo, z, hi, z], axis=-1)
    bot = jnp.concatenate([z, lo, z, hi], axis=-1)
    wp = jnp.concatenate([top, bot], axis=2)
    wc = wp.reshape(2, CMP_STRIDE // 2, 2 * 2 * HEAD_DIM, 4 * CMP_HIDDEN).astype(BF16)
    z2 = jnp.zeros((2, CMP_HIDDEN, HEAD_DIM), w2.dtype)
    w2bd = jnp.concatenate([jnp.concatenate([w2, z2], axis=-1), jnp.concatenate([z2, w2], axis=-1)], axis=1)
    w1f = w1.reshape(2, CMP_LEN * HEAD_DIM, CMP_HIDDEN).astype(BF16)
    pef = pe.reshape(2, 1, CMP_LEN * HEAD_DIM).astype(BF16)
    return wc, w1f, pef, w2bd.astype(BF16)


def _compress_prompt_kernel(x_ref, wc_ref, w1f_ref, pef_ref, w2_ref, kc_ref, vc_ref, *, nch):
    def load_xp(t, p):
        return x_ref[:, p, t * 2 * HEAD_DIM:(t + 1) * 2 * HEAD_DIM]
    kc, vc = _compress(load_xp, wc_ref, w1f_ref, pef_ref, w2_ref, nch)
    kc_ref[...] = kc
    vc_ref[...] = vc


def _compress_prompt(kvf, n_batch, seq, cw):
    nch = seq // CMP_STRIDE
    wc, w1f, pef, w2bd = cw
    out = jax.ShapeDtypeStruct((n_batch * nch, 2 * HEAD_DIM), F32)
    ospec = pl.BlockSpec((nch, 2 * HEAD_DIM), lambda b: (b, 0))
    return pl.pallas_call(
        functools.partial(_compress_prompt_kernel, nch=nch),
        grid=(n_batch,),
        in_specs=[pl.BlockSpec((nch, CMP_STRIDE, 4 * HEAD_DIM), lambda b: (b, 0, 0)),
                  _const_spec(wc.shape), _const_spec(w1f.shape), _const_spec(pef.shape), _const_spec(w2bd.shape)],
        out_specs=[ospec, ospec], out_shape=[out, out],
        compiler_params=_cparams("parallel"), name="compress_prompt")(
            kvf.reshape(n_batch * nch, CMP_STRIDE, kvf.shape[1]), wc, w1f, pef, w2bd)


NEAR_CMP = 16


def _cmp_select_kernel(q_ref, kc_ref, vc_ref, nt_ref, ovl_ref, gl_ref, o_ref, sel_ref, *, nch, n_blocks):
    i = pl.program_id(1)
    gates = jax.nn.sigmoid(gl_ref[...])
    cs = i * (Q_BLOCK // CMP_STRIDE) - NEAR_CMP // 2
    start = pl.multiple_of(jnp.maximum(cs, 0), 8)
    variant = jnp.minimum(i, 1)
    col = lax.broadcasted_iota(jnp.int32, (Q_BLOCK, nch), 1)
    far_pen = jnp.where(col < cs, 0.0, NEG)
    kc_all = kc_ref[...].astype(BF16)
    vc_all = vc_ref[...].astype(BF16)
    kc_near = kc_ref[pl.ds(start, NEAR_CMP), :].astype(BF16)
    vc_near = vc_ref[pl.ds(start, NEAR_CMP), :].astype(BF16)
    ovl_near = ovl_ref[pl.ds(start, NEAR_CMP), :]
    r_io = lax.broadcasted_iota(jnp.int32, (Q_BLOCK, n_blocks), 0)
    j_io = lax.broadcasted_iota(jnp.int32, (Q_BLOCK, n_blocks), 1)
    cur = (i * Q_BLOCK + r_io) // SEL_BLOCK
    forced = (j_io == 0) | (j_io == cur) | (j_io == cur - 1)
    sels = []
    for g in range(N_KV):
        qs = q_ref[g * GROUP:(g + 1) * GROUP].reshape(GROUP * Q_BLOCK, HEAD_DIM)
        cols = slice(g * HEAD_DIM, (g + 1) * HEAD_DIM)
        s_far = _slabs_add(_dot_nt(qs, kc_all[:, cols]), [far_pen] * GROUP)
        s_near = _slabs_add(_dot_nt(qs, kc_near[:, cols]),
                            [nt_ref[variant, g * GROUP + r] for r in range(GROUP)])
        m = jnp.maximum(jnp.max(s_far, axis=1, keepdims=True), jnp.max(s_near, axis=1, keepdims=True))
        m = jnp.where(m > 0.5 * NEG, m, 0.0)
        e_far = jnp.exp(s_far - m)
        e_near = jnp.exp(s_near - m)
        den = jnp.sum(e_far, axis=1, keepdims=True) + jnp.sum(e_near, axis=1, keepdims=True)
        inv = 1.0 / jnp.where(den > 0, den, 1.0)
        acc = _dot(e_far.astype(BF16), vc_all[:, cols]) + _dot(e_near.astype(BF16), vc_near[:, cols])
        p_far = e_far * inv
        p_near = e_near * inv
        pg_far = p_far[0:Q_BLOCK]
        pg_near = p_near[0:Q_BLOCK]
        for r in range(GROUP):
            h = g * GROUP + r
            rows = slice(r * Q_BLOCK, (r + 1) * Q_BLOCK)
            if r > 0:
                pg_far = pg_far + p_far[rows]
                pg_near = pg_near + p_near[rows]
            o_ref[:, h * HEAD_DIM:(h + 1) * HEAD_DIM] = acc[rows] * (inv[rows] * gates[:, h:h + 1])
        imp = _dot_f32_by_01(pg_far, ovl_ref[...]) + _dot_f32_by_01(pg_near, ovl_near)
        score = jnp.where(j_io <= cur, jnp.where(forced, FORCED_SCORE, imp), NEG)
        sels.append(_topk_mask(score, min(N_SELECT, n_blocks)))
    sel_ref[...] = jnp.concatenate(sels, axis=1).astype(BF16)


def _overlap_np(nch, n_blocks, n_cols):
    ci = np.arange(nch)[:, None] * CMP_STRIDE
    sj = np.arange(n_cols)[None, :] * SEL_BLOCK
    ov = (ci < sj + SEL_BLOCK) & (ci + CMP_LEN > sj) & (np.arange(n_cols)[None, :] < n_blocks)
    ov[nch - 1] = False
    return ov.astype(np.float32)


def _cmp_select(q, kc, vc, near_tiles, gate_logits, n_batch, seq):
    nq = seq // Q_BLOCK
    nch = seq // CMP_STRIDE
    n_blocks = seq // SEL_BLOCK
    n = n_batch * seq
    ovl = jnp.asarray(_overlap_np(nch, n_blocks, n_blocks), BF16)
    cspec = pl.BlockSpec((nch, 2 * HEAD_DIM), lambda b, i: (b, 0))
    rows = lambda w: pl.BlockSpec((Q_BLOCK, w), lambda b, i: (b * nq + i, 0))
    return pl.pallas_call(
        functools.partial(_cmp_select_kernel, nch=nch, n_blocks=n_blocks),
        grid=(n_batch, nq),
        in_specs=[pl.BlockSpec((N_HEADS, Q_BLOCK, HEAD_DIM), lambda b, i: (0, b * nq + i, 0)),
                  cspec, cspec, _const_spec(near_tiles.shape), _const_spec(ovl.shape),
                  rows(gate_logits.shape[1])],
        out_specs=[rows(HD), rows(N_KV * n_blocks)],
        out_shape=[jax.ShapeDtypeStruct((n, HD), F32), jax.ShapeDtypeStruct((n, N_KV * n_blocks), BF16)],
        compiler_params=_cparams("parallel", "parallel"), name="cmp_select")(
            q, kc, vc, near_tiles, ovl, gate_logits)


FAR_CHUNK = 512


def _selected_kernel(q_ref, k_ref, v_ref, sel_ref, ex_ref, bt_ref, gl_ref, o_ref,
                     mall_ref, m_ref, l_ref, acc_ref, *, n_blocks):
    i = pl.program_id(1)
    gates = jax.nn.sigmoid(gl_ref[...])
    far_end = (i - 1) * Q_BLOCK
    n_far = (i + 2) // (FAR_CHUNK // Q_BLOCK)
    for g in range(N_KV):
        qs = q_ref[g * GROUP:(g + 1) * GROUP].reshape(GROUP * Q_BLOCK, HEAD_DIM)
        cols = slice(g * HEAD_DIM, (g + 1) * HEAD_DIM)
        mall_ref[...] = _dot(sel_ref[:, g * n_blocks:(g + 1) * n_blocks], ex_ref[...])
        s_parts, v_parts = [], []
        for j in (1, 0):
            start = pl.multiple_of(jnp.maximum(i - j, 0) * Q_BLOCK, Q_BLOCK)
            v_parts.append(v_ref[pl.ds(start, Q_BLOCK), cols])
            mb = (mall_ref[:, pl.ds(start, Q_BLOCK)] - 1.0) * (-NEG)
            if j > 0:
                mb = mb + jnp.where(i >= j, 0.0, NEG)
            sj = _dot_nt(qs, k_ref[pl.ds(start, Q_BLOCK), cols])
            s_parts.append(_slabs_add(sj, [bt_ref[j, g * GROUP + r] + mb for r in range(GROUP)]))
        s = jnp.concatenate(s_parts, axis=1)
        m0 = jnp.max(s, axis=1, keepdims=True)
        e = jnp.exp(s - m0)
        m_ref[...] = m0
        l_ref[...] = jnp.sum(e, axis=1, keepdims=True)
        acc_ref[...] = _dot(e.astype(BF16), jnp.concatenate(v_parts, axis=0))

        def far_step(c, carry):
            start = pl.multiple_of(c * FAR_CHUNK, FAR_CHUNK)
            kc = k_ref[pl.ds(start, FAR_CHUNK), cols]
            vc = v_ref[pl.ds(start, FAR_CHUNK), cols]
            mb = (mall_ref[:, pl.ds(start, FAR_CHUNK)] - 1.0) * (-NEG)
            kpos = start + lax.broadcasted_iota(jnp.int32, (Q_BLOCK, FAR_CHUNK), 1)
            mb = jnp.where(kpos < far_end, mb, NEG)
            sc = _slabs_add(_dot_nt(qs, kc), [mb] * GROUP)
            m_old = m_ref[...]
            m_new = jnp.maximum(m_old, jnp.max(sc, axis=1, keepdims=True))
            alpha = jnp.exp(m_old - m_new)
            p = jnp.exp(sc - m_new)
            l_ref[...] = alpha * l_ref[...] + jnp.sum(p, axis=1, keepdims=True)
            acc_ref[...] = alpha * acc_ref[...] + _dot(p.astype(BF16), vc)
            m_ref[...] = m_new
            return carry

        lax.fori_loop(0, n_far, far_step, 0)
        acc = acc_ref[...]
        den = l_ref[...]
        for r in range(GROUP):
            h = g * GROUP + r
            rows = slice(r * Q_BLOCK, (r + 1) * Q_BLOCK)
            f = gates[:, N_HEADS + h:N_HEADS + h + 1] / den[rows]
            o_ref[:, h * HEAD_DIM:(h + 1) * HEAD_DIM] = acc[rows] * f


def _selected(q, kvb, sel, near_tiles, gate_logits, n_batch, seq):
    assert seq % FAR_CHUNK == 0
    nq = seq // Q_BLOCK
    n_blocks = seq // SEL_BLOCK
    n = n_batch * seq
    ex = (np.arange(n_blocks)[:, None] == (np.arange(seq)[None, :] // SEL_BLOCK)).astype(np.float32)
    ex = jnp.asarray(ex, BF16)
    rows = lambda w: pl.BlockSpec((Q_BLOCK, w), lambda b, i: (b * nq + i, 0))
    gq = GROUP * Q_BLOCK
    return pl.pallas_call(
        functools.partial(_selected_kernel, n_blocks=n_blocks),
        grid=(n_batch, nq),
        in_specs=[pl.BlockSpec((N_HEADS, Q_BLOCK, HEAD_DIM), lambda b, i: (0, b * nq + i, 0)),
                  pl.BlockSpec((seq, 2 * HEAD_DIM), lambda b, i: (b, 2)),
                  pl.BlockSpec((seq, 2 * HEAD_DIM), lambda b, i: (b, 3)),
                  rows(N_KV * n_blocks), _const_spec(ex.shape), _const_spec(near_tiles.shape),
                  rows(gate_logits.shape[1])],
        out_specs=rows(HD),
        out_shape=jax.ShapeDtypeStruct((n, HD), F32),
        scratch_shapes=[pltpu.VMEM((Q_BLOCK, seq), F32), pltpu.VMEM((gq, 1), F32),
                        pltpu.VMEM((gq, 1), F32), pltpu.VMEM((gq, HEAD_DIM), F32)],
        compiler_params=_cparams("parallel", "parallel"), name="selected_attn")(
            q, kvb, kvb, sel, ex, near_tiles, gate_logits)


def _block_diag_q(q):
    z = jnp.zeros_like(q)
    row = lax.broadcasted_iota(jnp.int32, (N_HEADS, 2 * HEAD_DIM), 0)
    return jnp.where(row < GROUP, jnp.concatenate([q, z], axis=1), jnp.concatenate([z, q], axis=1))


def _pick_group_half(o_full):
    row = lax.broadcasted_iota(jnp.int32, (N_HEADS, HEAD_DIM), 0)
    return jnp.where(row < GROUP, o_full[:, :HEAD_DIM], o_full[:, HEAD_DIM:])


def _window_decode_kernel(*refs, n_samp, use_sink, use_gate):
    it = iter(refs)
    q_ref, nkv_ref, st_ref, br_ref, b0_ref = next(it), next(it), next(it), next(it), next(it)
    sink_ref = next(it) if use_sink else None
    gl_ref = next(it) if use_gate else None
    o_ref = next(it)
    for s_i in range(n_samp):
        qbd = _block_diag_q(q_ref[s_i])
        kt = st_ref[s_i, 0, 0].astype(BF16)
        vt = st_ref[s_i, 0, 1].astype(BF16)
        new = nkv_ref[s_i]
        s_buf = _dot(qbd, kt) + br_ref[...]
        s_new = jnp.sum(qbd.astype(F32) * new[:, 0:2 * HEAD_DIM], axis=1, keepdims=True) + b0_ref[...]
        m = jnp.maximum(jnp.max(s_buf, axis=1, keepdims=True), s_new)
        if use_sink:
            m = jnp.maximum(m, sink_ref[...])
        e_buf = jnp.exp(s_buf - m)
        e_new = jnp.exp(s_new - m)
        den = jnp.sum(e_buf, axis=1, keepdims=True) + e_new
        if use_sink:
            den = den + jnp.exp(sink_ref[...] - m)
        o_full = _dot_nt(e_buf.astype(BF16), vt) + e_new * new[:, 2 * HEAD_DIM:4 * HEAD_DIM]
        f = 1.0 / den
        if use_gate:
            f = f * jax.nn.sigmoid(gl_ref[s_i])
        o_ref[s_i] = _pick_group_half(o_full) * f


def _window_decode(q, new_kv, state, layer, bias_row, bias0, sinks, gate_logits):
    db, lb = state.shape[0], state.shape[-1]
    n_samp = 8 if db % 8 == 0 else 1
    use_sink = sinks is not None
    use_gate = gate_logits is not None
    ins = [q, new_kv.reshape(db, 1, 4 * HEAD_DIM), state, bias_row, bias0]
    in_specs = [pl.BlockSpec((n_samp, N_HEADS, HEAD_DIM), lambda t: (t, 0, 0)),
                pl.BlockSpec((n_samp, 1, 4 * HEAD_DIM), lambda t: (t, 0, 0)),
                pl.BlockSpec((n_samp, 1, 2, 2 * HEAD_DIM, lb), lambda t: (t, layer, 0, 0, 0)),
                _const_spec(bias_row.shape), _const_spec(bias0.shape)]
    if use_sink:
        ins.append(sinks.astype(F32).reshape(N_HEADS, 1))
        in_specs.append(_const_spec((N_HEADS, 1)))
    if use_gate:
        ins.append(gate_logits)
        in_specs.append(pl.BlockSpec((n_samp, N_HEADS, 1), lambda t: (t, 0, 0)))
    return pl.pallas_call(
        functools.partial(_window_decode_kernel, n_samp=n_samp, use_sink=use_sink, use_gate=use_gate),
        grid=(db // n_samp,), in_specs=in_specs,
        out_specs=pl.BlockSpec((n_samp, N_HEADS, HEAD_DIM), lambda t: (t, 0, 0)),
        out_shape=jax.ShapeDtypeStruct((db, N_HEADS, HEAD_DIM), F32),
        compiler_params=_cparams("parallel"), name="window_decode")(*ins)


N_KIND = 4
XT_ROWS = 1024


def _nsa_decode_kernel(pt_ref, q_ref, nkv_ref, gl_ref, pool_ref, wc_ref, w1f_ref, pef_ref, w2_ref,
                       cb_ref, sb_ref, b0_ref, ovl_ref, ex_ref, oc_ref, os_ref,
                       buf_a, buf_b, xrow, sem, *, layer, n_pages, past, n_blocks, nbp):
    step = pl.program_id(0)
    n_steps = pl.num_programs(0)
    nch = past // CMP_STRIDE
    n_past = past // SEL_BLOCK

    def page_copy(page, j, buf, slot):
        return pltpu.make_async_copy(
            pool_ref.at[page, layer], buf.at[:, :, pl.ds(j * PAGE_SIZE, PAGE_SIZE)], sem.at[slot])

    def start_fetch(sample, buf, slot):
        def body(j, c):
            page_copy(pt_ref[sample * n_pages + j], j, buf, slot).start()
            return c
        lax.fori_loop(0, n_pages, body, 0)

    def wait_fetch(buf, slot):
        def body(j, c):
            page_copy(0, j, buf, slot).wait()
            return c
        lax.fori_loop(0, n_pages, body, 0)

    def compute(buf, s_i):
        qbd = _block_diag_q(q_ref[s_i])
        new = nkv_ref[s_i]
        gates = jax.nn.sigmoid(gl_ref[s_i])

        def to_rows(t):
            xr = min(XT_ROWS, past)
            cpx = xr // CMP_STRIDE
            for c in range(past // xr):
                xt = buf[t, :, c * xr:(c + 1) * xr]
                xrow[c * cpx:(c + 1) * cpx] = xt.T.reshape(cpx, CMP_STRIDE, 2 * HEAD_DIM)

        def load_xp(t, p):
            return xrow[:, p, :]
        kc, vc = _compress(load_xp, wc_ref, w1f_ref, pef_ref, w2_ref, nch, prepare=to_rows)
        s = _dot_nt(qbd, kc.astype(BF16)) + cb_ref[...]
        m = jnp.max(s, axis=1, keepdims=True)
        m = jnp.where(m > 0.5 * NEG, m, 0.0)
        e = jnp.exp(s - m)
        den = jnp.sum(e, axis=1, keepdims=True)
        inv = 1.0 / jnp.where(den > 0, den, 1.0)
        oc_full = _dot(e.astype(BF16), vc.astype(BF16))
        oc_ref[s_i] = _pick_group_half(oc_full) * (inv * gates[0])
        p = e * inv
        j_io = lax.broadcasted_iota(jnp.int32, (1, nbp), 1)
        cur = past // SEL_BLOCK
        forced = (j_io == 0) | (j_io == cur) | (j_io == cur - 1)
        sel_rows = []
        for g in range(N_KV):
            pg = jnp.sum(p[g * GROUP:(g + 1) * GROUP], axis=0, keepdims=True)
            imp = _dot_f32_by_01(pg, ovl_ref[...])
            score = jnp.where(j_io <= cur, jnp.where(forced, FORCED_SCORE, imp), NEG)
            sel_rows.append(_topk_mask(score, min(N_SELECT, n_blocks))[:, 0:n_past])
        row = lax.broadcasted_iota(jnp.int32, (N_HEADS, n_past), 0)
        sel16 = jnp.where(row < GROUP, sel_rows[0], sel_rows[1]).astype(BF16)
        selm = _dot(sel16, ex_ref[...])
        s2 = _dot(qbd, buf[2].astype(BF16)) + sb_ref[...] + (selm - 1.0) * (-NEG)
        s_new = jnp.sum(qbd.astype(F32) * new[:, 0:2 * HEAD_DIM], axis=1, keepdims=True) + b0_ref[...]
        m2 = jnp.maximum(jnp.max(s2, axis=1, keepdims=True), s_new)
        e2 = jnp.exp(s2 - m2)
        e_new = jnp.exp(s_new - m2)
        den2 = jnp.sum(e2, axis=1, keepdims=True) + e_new
        os_full = _dot_nt(e2.astype(BF16), buf[3].astype(BF16)) + e_new * new[:, 2 * HEAD_DIM:4 * HEAD_DIM]
        os_ref[s_i] = _pick_group_half(os_full) * (gates[1] / den2)

    @pl.when(step == 0)
    def _():
        start_fetch(0, buf_a, 0)

    start_fetch(2 * step + 1, buf_b, 1)
    wait_fetch(buf_a, 0)
    compute(buf_a, 0)

    @pl.when(step + 1 < n_steps)
    def _():
        start_fetch(2 * step + 2, buf_a, 0)

    wait_fetch(buf_b, 1)
    compute(buf_b, 1)


def _nsa_decode(q, new_sel, gate_logits2, pool, page_table, layer, cw, cmp_bias, sel_bias, bias0):
    db, n_pages = page_table.shape
    assert db % 2 == 0
    past = n_pages * PAGE_SIZE
    nch = past // CMP_STRIDE
    n_past = past // SEL_BLOCK
    n_blocks = n_past + 1
    nbp = -(-n_blocks // 128) * 128
    wc, w1f, pef, w2bd = cw
    ovl = jnp.asarray(_overlap_np(nch, n_blocks, nbp), BF16)
    ex = jnp.asarray((np.arange(n_past)[:, None] == (np.arange(past)[None, :] // SEL_BLOCK)).astype(np.float32), BF16)
    consts = [wc, w1f, pef, w2bd, cmp_bias, sel_bias, bias0, ovl, ex]
    per2 = lambda *tail: pl.BlockSpec((2,) + tail, lambda t, pt: (t,) + (0,) * len(tail))
    grid_spec = pltpu.PrefetchScalarGridSpec(
        num_scalar_prefetch=1, grid=(db // 2,),
        in_specs=[per2(N_HEADS, HEAD_DIM), per2(1, 4 * HEAD_DIM), per2(2, N_HEADS, 1),
                  pl.BlockSpec(memory_space=pl.ANY)] + [_const_spec(c.shape) for c in consts],
        out_specs=[per2(N_HEADS, HEAD_DIM), per2(N_HEADS, HEAD_DIM)],
        scratch_shapes=[pltpu.VMEM((N_KIND, 2 * HEAD_DIM, past), F32), pltpu.VMEM((N_KIND, 2 * HEAD_DIM, past), F32),
                        pltpu.VMEM((nch, CMP_STRIDE, 2 * HEAD_DIM), F32), pltpu.SemaphoreType.DMA((2,))])
    out = jax.ShapeDtypeStruct((db, N_HEADS, HEAD_DIM), F32)
    return pl.pallas_call(
        functools.partial(_nsa_decode_kernel, layer=layer, n_pages=n_pages, past=past,
                          n_blocks=n_blocks, nbp=nbp),
        grid_spec=grid_spec, out_shape=[out, out],
        compiler_params=_cparams("arbitrary"), name="nsa_decode")(
            page_table.reshape(-1), q, new_sel.reshape(db, 1, 4 * HEAD_DIM), gate_logits2, pool, *consts)


def _shift_kernel(st_ref, new_ref, o_ref, *, lb):
    x = st_ref[...]
    lane = lax.broadcasted_iota(jnp.int32, x.shape, 2)
    o_ref[...] = jnp.where(lane == lb - 1, new_ref[...], pltpu.roll(x, lb - 1, 2))


def _shift_state(state, new_rows):
    db, w, lb = state.shape
    ns = max(1, min(db, (4 << 20) // (w * lb * 4)))
    assert db % ns == 0
    blk = pl.BlockSpec((ns, w, lb), lambda t: (t, 0, 0))
    return pl.pallas_call(
        functools.partial(_shift_kernel, lb=lb), grid=(db // ns,),
        in_specs=[blk, pl.BlockSpec((ns, w, 1), lambda t: (t, 0, 0))], out_specs=blk,
        out_shape=jax.ShapeDtypeStruct((db, w, lb), state.dtype),
        compiler_params=_cparams("parallel"), name="shift_state")(state, new_rows.reshape(db, w, 1))


def _decode_bias_row(rel_bias, past, kpos, window):
    dist = past - kpos
    valid = (dist >= 0) if window is None else ((dist >= 0) & (dist < window))
    return _bias_tile(rel_bias, dist[None, :], valid[None, :], False)[:, 0, :]


def kernel(x_prompt, x_sample, state_swa, cache_nsa, state_nsa_win, page_table, rel_bias, norm_mix, norm_mlp,
           norm_final, w_in_a, b_in_a, w_out_a, b_out_a, sinks_a, w_in_b, w_out_b, w_cmp1, w_cmp2, pe_cmp,
           w_up, w_down):
    n_batch, seq, _ = x_prompt.shape
    db = x_sample.shape[0]
    assert x_sample.shape[1] == 1
    depth = norm_mix.shape[0]
    n_la, n_lb = w_in_a.shape[0], w_in_b.shape[0]
    lba, lbb = state_swa.shape[1], state_nsa_win.shape[1]
    n_pages = page_table.shape[1]
    past = n_pages * PAGE_SIZE
    assert seq >= WIN_B and lba == WIN_A and lbb == WIN_B and past >= WIN_B

    r = np.arange(Q_BLOCK)[:, None]
    c = np.arange(Q_BLOCK)[None, :]
    d0, d1 = r - c, Q_BLOCK + r - c
    swa_tiles = jnp.stack([_bias_tile(rel_bias, d0, (d0 >= 0) & (d0 < WIN_A), False),
                           _bias_tile(rel_bias, d1, (d1 >= 0) & (d1 < WIN_A), False)])
    near_tiles = jnp.stack([_bias_tile(rel_bias, d0, d0 >= 0, True),
                            _bias_tile(rel_bias, d1, d1 >= 0, True)])
    nback_b = WIN_B // Q_BLOCK
    far_mask = jnp.asarray(np.where(c > r, 0.0, NEG), F32)
    no_mask = jnp.zeros((8, 128), F32)
    cc = np.arange(NEAR_CMP)[None, :]
    dc0 = r - CMP_STRIDE * cc - (CMP_LEN - 1)
    dc1 = r - CMP_STRIDE * (cc - NEAR_CMP // 2) - (CMP_LEN - 1)
    cmp_tiles = jnp.stack([_bias_tile(rel_bias, dc0, dc0 >= 0, True), _bias_tile(rel_bias, dc1, dc1 >= 0, True)])
    nch_d = past // CMP_STRIDE
    cmp_row = _decode_bias_row(rel_bias, past, np.arange(nch_d) * CMP_STRIDE + CMP_LEN - 1, None)
    sel_row = _decode_bias_row(rel_bias, past, np.arange(past), None)
    swa_row = _decode_bias_row(rel_bias, past, past - lba + np.arange(lba), WIN_A)
    win_row = _decode_bias_row(rel_bias, past, past - lbb + np.arange(lbb), WIN_B)
    bias0 = rel_bias.astype(F32)[0].reshape(N_HEADS, 1)

    xp = x_prompt.reshape(n_batch * seq, D_MODEL)
    xs = x_sample.reshape(db, D_MODEL)
    to_minor = lambda a: jnp.transpose(a, (0, 2, 3, 4, 5, 1))
    st_swa = to_minor(state_swa).reshape(db, n_la, 2, KVD, lba)
    st_win = to_minor(state_nsa_win).reshape(db, n_lb, 2, KVD, lbb)
    pool = to_minor(cache_nsa).reshape(cache_nsa.shape[0], n_lb, N_KIND, KVD, PAGE_SIZE)
    swa_p, swa_new, rows_p, rows_s, win_p, win_new = [], [], [], [], [], []

    for layer in range(depth):
        final = norm_final if layer == depth - 1 else None
        if layer % 2 == 0:
            a = layer // 2
            q_p, kv_p, kvb_p, _ = _project(xp, norm_mix[layer], w_in_a[a], b_in_a[a], 2 * KVD)
            q_s, kv_s, _, _ = _project(xs, norm_mix[layer], w_in_a[a], b_in_a[a], 2 * KVD)
            o_p = _banded(q_p, kvb_p, 0, 1, n_batch, seq, ("bias", "bias"), swa_tiles, no_mask,
                          sinks_a[a], None, None)
            o_s = _window_decode(jnp.swapaxes(q_s, 0, 1), kv_s, st_swa, a, swa_row, bias0, sinks_a[a], None)
            swa_p.append(kv_p.reshape(n_batch, seq, 2, N_KV, HEAD_DIM)[:, seq - WIN_A:])
            swa_new.append(kv_s)
            xp = _out_mlp(xp, [o_p], w_out_a[a], b_out_a[a], norm_mlp[layer], w_up[layer], w_down[layer], final)
            xs = _out_mlp(xs, [o_s.reshape(db, HD)], w_out_a[a], b_out_a[a], norm_mlp[layer], w_up[layer],
                          w_down[layer], final)
        else:
            b = layer // 2
            cw = _compress_weights(w_cmp1[b], w_cmp2[b], pe_cmp[b])
            q_p, kv_p, kvb_p, gl_p = _project(xp, norm_mix[layer], w_in_b[b], None, 6 * KVD)
            q_s, kv_s, _, gl_s = _project(xs, norm_mix[layer], w_in_b[b], None, 6 * KVD)
            kc, vc = _compress_prompt(kv_p, n_batch, seq, cw)
            o_c, sel = _cmp_select(q_p, kc, vc, cmp_tiles, gl_p, n_batch, seq)
            o_sel = _selected(q_p, kvb_p, sel, near_tiles, gl_p, n_batch, seq)
            kinds = ("bias", "bias") + ("none",) * (nback_b - 2) + ("mask",)
            o_w = _banded(q_p, kvb_p, 4, 5, n_batch, seq, kinds, near_tiles, far_mask, None, gl_p, 2 * N_HEADS)
            rows_p.append(kv_p[:, :4 * KVD].reshape(n_batch, seq, 4, N_KV, HEAD_DIM))
            win_p.append(kv_p.reshape(n_batch, seq, 6 * KVD)[:, seq - WIN_B:, 4 * KVD:]
                         .reshape(n_batch, WIN_B, 2, N_KV, HEAD_DIM))
            xp = _out_mlp(xp, [o_c, o_sel, o_w], w_out_b[b], None, norm_mlp[layer], w_up[layer], w_down[layer], final)
            q_sd = jnp.swapaxes(q_s, 0, 1)
            gl_s3 = gl_s.reshape(db, 3, N_HEADS, 1)
            oc_s, os_s = _nsa_decode(q_sd, kv_s[:, 2 * KVD:4 * KVD], gl_s3[:, 0:2], pool, page_table, b, cw,
                                     cmp_row, sel_row, bias0)
            ow_s = _window_decode(q_sd, kv_s[:, 4 * KVD:], st_win, b, win_row, bias0, None, gl_s3[:, 2])
            rows_s.append(kv_s[:, :4 * KVD].reshape(db, 1, 4, N_KV, HEAD_DIM))
            win_new.append(kv_s[:, 4 * KVD:])
            xs = _out_mlp(xs, [oc_s.reshape(db, HD), os_s.reshape(db, HD), ow_s.reshape(db, HD)], w_out_b[b], None,
                          norm_mlp[layer], w_up[layer], w_down[layer], final)

    y_prompt = xp.reshape(n_batch, seq, D_MODEL)
    y_sample = xs.reshape(db, 1, D_MODEL)
    from_minor = lambda a, nl, lb: jnp.transpose(a.reshape(db, nl, 2, N_KV, HEAD_DIM, lb), (0, 5, 1, 2, 3, 4))
    swa_sample = _shift_state(st_swa.reshape(db, n_la * 2 * KVD, lba), jnp.concatenate(swa_new, axis=1))
    win_sample = _shift_state(st_win.reshape(db, n_lb * 2 * KVD, lbb), jnp.concatenate(win_new, axis=1))
    return (y_prompt, y_sample,
            jnp.stack(swa_p, axis=2),
            from_minor(swa_sample, n_la, lba),
            jnp.stack(rows_p, axis=2), jnp.stack(rows_s, axis=2),
            jnp.stack(win_p, axis=2),
            from_minor(win_sample, n_lb, lbb))
```

```python
import functools
import math

import numpy as np
import jax
import jax.numpy as jnp
from jax import lax
from jax.experimental import pallas as pl
from jax.experimental.pallas import tpu as pltpu

D_MODEL = 1024
N_HEADS = 16
HEAD_DIM = 64
N_KV = 2
GROUP = N_HEADS // N_KV
HD = N_HEADS * HEAD_DIM
KVD = N_KV * HEAD_DIM
D_FF = 4 * D_MODEL
PAGE_SIZE = 128
WIN_A = 128
WIN_B = 512
Q_BLOCK = 128
CMP_STRIDE = 16
CMP_LEN = 2 * CMP_STRIDE
CMP_HIDDEN = 128
SEL_BLOCK = 64
N_SELECT = 16
FORCED_SCORE = 1e9
N_BUCKETS = 32
MAX_DISTANCE = 128
NORM_EPS = 1e-5
SCALE = HEAD_DIM ** -0.5
N_GATE = 3 * N_HEADS
LANE = 128

NEG = -1e30
REMOVED = -2e30
F32 = jnp.float32
BF16 = jnp.bfloat16
VMEM_LIMIT = 56 * 1024 * 1024


def _cparams(*sem):
    return pltpu.CompilerParams(dimension_semantics=sem, vmem_limit_bytes=VMEM_LIMIT)


def _const_spec(shape):
    nd = len(shape)
    return pl.BlockSpec(shape, lambda *_: (0,) * nd, pipeline_mode=pl.Buffered(1))


def _bucket_np(dist):
    n = np.maximum(dist, 0)
    exact = N_BUCKETS // 2
    nf = np.maximum(n, exact).astype(np.float32)
    large = exact + (np.log(nf / np.float32(exact)) / np.float32(math.log(MAX_DISTANCE / exact))
                     * np.float32(N_BUCKETS - exact)).astype(np.int32)
    return np.where(n < exact, n, np.minimum(large, N_BUCKETS - 1)).astype(np.int32)


def _bias_tile(rel_bias, dist, valid, shift):
    tb = rel_bias.astype(F32)
    if shift:
        tb = tb - tb[N_BUCKETS - 1][None, :]
    bucket = jnp.asarray(_bucket_np(dist).reshape(1, -1))
    onehot = (bucket == jnp.arange(N_BUCKETS, dtype=jnp.int32)[:, None]).astype(F32)
    b = jnp.dot(tb.T, onehot, precision=lax.Precision.HIGHEST).reshape((N_HEADS,) + dist.shape)
    return jnp.where(jnp.asarray(valid)[None], b, NEG)


def _rms(x, g):
    ms = jnp.mean(x * x, axis=-1, keepdims=True)
    return x * lax.rsqrt(ms + NORM_EPS) * g


def _dot(a, b):
    return jnp.dot(a, b, preferred_element_type=F32)


def _dot_nt(a, b):
    return lax.dot_general(a, b, (((1,), (1,)), ((), ())), preferred_element_type=F32)


def _split3(a):
    hi = a.astype(BF16)
    r1 = a - hi.astype(F32)
    mid = r1.astype(BF16)
    lo = (r1 - mid.astype(F32)).astype(BF16)
    return hi, mid, lo


def _dot_f32_by_01(a, b01):
    hi, mid, lo = _split3(a)
    return _dot(hi, b01) + _dot(mid, b01) + _dot(lo, b01)


def _dot_01_by_f32(b01, a):
    hi, mid, lo = _split3(a)
    return _dot(b01, hi) + _dot(b01, mid) + _dot(b01, lo)


def _topk_mask(score, k, axis):
    n = score.shape[axis]
    iota = lax.broadcasted_iota(jnp.int32, score.shape, axis).astype(F32)
    sel = jnp.zeros_like(score)
    s = score
    for _ in range(k):
        m = jnp.max(s, axis=axis, keepdims=True)
        first = jnp.min(jnp.where(s == m, iota, float(n)), axis=axis, keepdims=True)
        hit = iota == first
        ok = jnp.where(m > 0.5 * NEG, 1.0, 0.0)
        sel = jnp.where(hit, ok, sel)
        s = jnp.where(hit, REMOVED, s)
    return sel


def _topk_mask_row(score, k):
    n = score.shape[1]
    s_j = jnp.broadcast_to(score, (n, n))
    s_i = jnp.concatenate([jnp.broadcast_to(score, (LANE, n)).T] * (n // LANE), axis=1)
    i_io = lax.broadcasted_iota(jnp.int32, (n, n), 0)
    j_io = lax.broadcasted_iota(jnp.int32, (n, n), 1)
    ahead = jnp.where(s_i > s_j, 1.0, jnp.where(s_i == s_j, jnp.where(i_io < j_io, 1.0, 0.0), 0.0))
    rank = jnp.sum(ahead, axis=0, keepdims=True)
    return jnp.where(rank < k, jnp.where(score > 0.5 * NEG, 1.0, 0.0), 0.0)


def _q_group_t(qt_ref, g):
    return jnp.concatenate(
        [qt_ref[(g * GROUP + r) * HEAD_DIM:(g * GROUP + r + 1) * HEAD_DIM, :] for r in range(GROUP)], axis=1)


def _heads_add(s, tiles):
    return jnp.concatenate([s[:, r * Q_BLOCK:(r + 1) * Q_BLOCK] + tiles[r] for r in range(GROUP)], axis=1)


def _head_row(rows):
    return jnp.concatenate(rows, axis=1)


def _head_slabs(acc_t):
    return [acc_t[:, r * Q_BLOCK:(r + 1) * Q_BLOCK] for r in range(GROUP)]


def _proj_kernel(*refs, n_kv, has_bias, has_gate, v_cols):
    it = iter(refs)
    x_ref, g_ref, w_ref = next(it), next(it), next(it)
    b_ref = next(it) if has_bias else None
    q_ref, kv_ref, kvb_ref = next(it), next(it), next(it)
    vt_ref = next(it) if v_cols is not None else None
    gl_ref = next(it) if has_gate else None
    xn = _rms(x_ref[...], g_ref[...]).astype(BF16)
    u = _dot(xn, w_ref[...])
    if has_bias:
        u = u + b_ref[...]
    kv = u[:, HD:HD + n_kv]
    kv_ref[...] = kv
    kvb_ref[...] = kv.astype(BF16)
    if v_cols is None:
        for h in range(N_HEADS):
            q_ref[h] = (u[:, h * HEAD_DIM:(h + 1) * HEAD_DIM] * SCALE).astype(BF16)
        if has_gate:
            gl_ref[...] = u[:, HD + n_kv:HD + n_kv + N_GATE]
    else:
        q_ref[...] = (u[:, :HD] * SCALE).T.astype(BF16)
        for idx, c0 in enumerate(v_cols):
            vt_ref[idx * KVD:(idx + 1) * KVD, :] = u[:, HD + c0:HD + c0 + KVD].T.astype(BF16)
        if has_gate:
            gl_ref[...] = u[:, HD + n_kv:HD + n_kv + LANE].T


def _project(x, g, w, b, n_kv, has_gate, v_cols):
    n = x.shape[0]
    dout = w.shape[1]
    tm = min(512, n)
    assert n % tm == 0
    has_bias = b is not None
    prompt = v_cols is not None
    ins = [x, g.reshape(1, D_MODEL), w.astype(BF16)]
    in_specs = [pl.BlockSpec((tm, D_MODEL), lambda t: (t, 0)),
                _const_spec((1, D_MODEL)), _const_spec((D_MODEL, dout))]
    if has_bias:
        ins.append(b.reshape(1, dout))
        in_specs.append(_const_spec((1, dout)))
    rows = lambda wd, dt: (jax.ShapeDtypeStruct((n, wd), dt), pl.BlockSpec((tm, wd), lambda t: (t, 0)))
    cols = lambda ht, dt: (jax.ShapeDtypeStruct((ht, n), dt), pl.BlockSpec((ht, tm), lambda t: (0, t)))
    if prompt:
        outs = [cols(HD, BF16), rows(n_kv, F32), rows(n_kv, BF16), cols(len(v_cols) * KVD, BF16)]
        if has_gate:
            outs.append(cols(LANE, F32))
    else:
        outs = [(jax.ShapeDtypeStruct((N_HEADS, n, HEAD_DIM), BF16),
                 pl.BlockSpec((N_HEADS, tm, HEAD_DIM), lambda t: (0, t, 0))),
                rows(n_kv, F32), rows(n_kv, BF16)]
        if has_gate:
            outs.append(rows(N_GATE, F32))
    return pl.pallas_call(
        functools.partial(_proj_kernel, n_kv=n_kv, has_bias=has_bias, has_gate=has_gate, v_cols=v_cols),
        grid=(n // tm,), in_specs=in_specs, out_specs=[o[1] for o in outs], out_shape=[o[0] for o in outs],
        compiler_params=_cparams("parallel"), name="norm_proj")(*ins)


FF_CHUNK = 1024


def _out_mlp_kernel(*refs, n_o, has_bias, final):
    it = iter(refs)
    x_ref = next(it)
    o_refs = [next(it) for _ in range(n_o)]
    wo_ref = next(it)
    bo_ref = next(it) if has_bias else None
    gm_ref, wup_ref, wdn_ref = next(it), next(it), next(it)
    gf_ref = next(it) if final else None
    y_ref = next(it)
    o = o_refs[0][...]
    for r in o_refs[1:]:
        o = o + r[...]
    x1 = x_ref[...] + _dot(o.astype(BF16), wo_ref[...])
    if has_bias:
        x1 = x1 + bo_ref[...]
    xn = _rms(x1, gm_ref[...]).astype(BF16)
    acc = x1
    for c in range(D_FF // FF_CHUNK):
        h = jnp.maximum(_dot(xn, wup_ref[:, c * FF_CHUNK:(c + 1) * FF_CHUNK]), 0.0)
        acc = acc + _dot((h * h).astype(BF16), wdn_ref[c * FF_CHUNK:(c + 1) * FF_CHUNK, :])
    y_ref[...] = _rms(acc, gf_ref[...]) if final else acc


def _out_mlp(x, os_, w_out, b_out, g_mlp, w_up, w_down, g_final):
    n = x.shape[0]
    tm = min(256, n)
    assert n % tm == 0
    has_bias = b_out is not None
    final = g_final is not None
    row = pl.BlockSpec((tm, D_MODEL), lambda t: (t, 0))
    ins = [x, *os_, w_out.astype(BF16)]
    in_specs = [row] + [row] * len(os_) + [_const_spec((HD, D_MODEL))]
    if has_bias:
        ins.append(b_out.reshape(1, D_MODEL))
        in_specs.append(_const_spec((1, D_MODEL)))
    ins += [g_mlp.reshape(1, D_MODEL), w_up.astype(BF16), w_down.astype(BF16)]
    in_specs += [_const_spec((1, D_MODEL)), _const_spec((D_MODEL, D_FF)), _const_spec((D_FF, D_MODEL))]
    if final:
        ins.append(g_final.reshape(1, D_MODEL))
        in_specs.append(_const_spec((1, D_MODEL)))
    return pl.pallas_call(
        functools.partial(_out_mlp_kernel, n_o=len(os_), has_bias=has_bias, final=final),
        grid=(n // tm,), in_specs=in_specs, out_specs=row,
        out_shape=jax.ShapeDtypeStruct((n, D_MODEL), F32),
        compiler_params=_cparams("parallel"), name="out_mlp")(*ins)


def _qblock_cols(height, nq):
    return pl.BlockSpec((height, Q_BLOCK), lambda b, i: (0, b * nq + i))


def _qblock_rows(width, nq):
    return pl.BlockSpec((Q_BLOCK, width), lambda b, i: (b * nq + i, 0))


def _gate_row(gates, branch, g):
    return _head_row([gates[branch * N_HEADS + g * GROUP + r:branch * N_HEADS + g * GROUP + r + 1, :]
                      for r in range(GROUP)])


def _banded_kernel(*refs, kinds, use_sink, gate_branch):
    it = iter(refs)
    qt_ref, k_ref, vt_ref, bt_ref, mt_ref = next(it), next(it), next(it), next(it), next(it)
    sink_ref = next(it) if use_sink else None
    gl_ref = next(it) if gate_branch is not None else None
    o_ref = next(it)
    i = pl.program_id(1)
    gates = jax.nn.sigmoid(gl_ref[0:N_GATE, :]) if gate_branch is not None else None
    bias_slot = {}
    for j, kind in enumerate(kinds):
        if kind == "bias":
            bias_slot[j] = len(bias_slot)
    outs = []
    for g in range(N_KV):
        qt = _q_group_t(qt_ref, g)
        gsl = slice(g * HEAD_DIM, (g + 1) * HEAD_DIM)
        s_parts, v_parts = [], []
        for j, kind in enumerate(kinds):
            start = pl.multiple_of(jnp.maximum(i - j, 0) * Q_BLOCK, Q_BLOCK)
            v_parts.append(vt_ref[gsl, pl.ds(start, Q_BLOCK)])
            sj = _dot(k_ref[pl.ds(start, Q_BLOCK), gsl], qt)
            pen = jnp.where(i >= j, 0.0, NEG) if j > 0 else 0.0
            if kind == "bias":
                sj = _heads_add(sj, [bt_ref[bias_slot[j], g * GROUP + r] + pen for r in range(GROUP)])
            elif kind == "mask":
                sj = _heads_add(sj, [mt_ref[...] + pen] * GROUP)
            else:
                sj = sj + pen
            s_parts.append(sj)
        s = jnp.concatenate(s_parts, axis=0)
        vt = jnp.concatenate(v_parts, axis=1)
        m = jnp.max(s, axis=0, keepdims=True)
        if use_sink:
            sk = _head_row([jnp.full((1, Q_BLOCK), sink_ref[g * GROUP + r], F32) for r in range(GROUP)])
            m = jnp.maximum(m, sk)
        e = jnp.exp(s - m)
        den = jnp.sum(e, axis=0, keepdims=True)
        if use_sink:
            den = den + jnp.exp(sk - m)
        f = 1.0 / den
        if gates is not None:
            f = f * _gate_row(gates, gate_branch, g)
        outs += _head_slabs(_dot(vt, e.astype(BF16)) * f)
    o_ref[...] = jnp.concatenate(outs, axis=0).T


def _banded(qt, kvb, k_col, vt, v_idx, n_batch, seq, kinds, bias_tiles, mask_tile, sinks, gate_t, gate_branch):
    nq = seq // Q_BLOCK
    use_sink = sinks is not None
    ins = [qt, kvb, vt, bias_tiles, mask_tile]
    in_specs = [_qblock_cols(HD, nq),
                pl.BlockSpec((seq, KVD), lambda b, i: (b, k_col)),
                pl.BlockSpec((KVD, seq), lambda b, i: (v_idx, b)),
                _const_spec(bias_tiles.shape), _const_spec(mask_tile.shape)]
    if use_sink:
        ins.append(sinks.astype(F32))
        in_specs.append(pl.BlockSpec(memory_space=pltpu.SMEM))
    if gate_t is not None:
        ins.append(gate_t)
        in_specs.append(_qblock_cols(LANE, nq))
    else:
        gate_branch = None
    return pl.pallas_call(
        functools.partial(_banded_kernel, kinds=kinds, use_sink=use_sink, gate_branch=gate_branch),
        grid=(n_batch, nq), in_specs=in_specs, out_specs=_qblock_rows(HD, nq),
        out_shape=jax.ShapeDtypeStruct((n_batch * seq, HD), F32),
        compiler_params=_cparams("parallel", "parallel"), name="banded_attn")(*ins)


def _compress(load_xp, wc_ref, w1f_ref, pef_ref, w2_ref, nch, prepare=None):
    outs = []
    for t in range(2):
        if prepare is not None:
            prepare(t)
        xs = jnp.concatenate([load_xp(t, p).astype(BF16) for p in range(CMP_STRIDE)], axis=1)
        acc = _dot(xs, wc_ref[t])
        pet = _dot(pef_ref[t], w1f_ref[t])
        lo = acc[:, :2 * CMP_HIDDEN]
        hi_next = pltpu.roll(acc[:, 2 * CMP_HIDDEN:], nch - 1, 0)
        hh = lo + hi_next + jnp.concatenate([pet, pet], axis=1)
        a = (hh * jax.nn.sigmoid(hh)).astype(BF16)
        outs.append(_dot(a, w2_ref[t]))
    return outs


def _compress_weights(w1, w2, pe):
    z = jnp.zeros((2, CMP_STRIDE, HEAD_DIM, CMP_HIDDEN), w1.dtype)
    lo, hi = w1[:, :CMP_STRIDE], w1[:, CMP_STRIDE:]
    top = jnp.concatenate([lo, z, hi, z], axis=-1)
    bot = jnp.concatenate([z, lo, z, hi], axis=-1)
    wp = jnp.concatenate([top, bot], axis=2)
    wc = wp.reshape(2, CMP_STRIDE * KVD, 4 * CMP_HIDDEN).astype(BF16)
    z2 = jnp.zeros((2, CMP_HIDDEN, HEAD_DIM), w2.dtype)
    w2bd = jnp.concatenate([jnp.concatenate([w2, z2], axis=-1), jnp.concatenate([z2, w2], axis=-1)], axis=1)
    w1f = w1.reshape(2, CMP_LEN * HEAD_DIM, CMP_HIDDEN).astype(BF16)
    pef = pe.reshape(2, 1, CMP_LEN * HEAD_DIM).astype(BF16)
    return wc, w1f, pef, w2bd.astype(BF16)


def _compress_prompt_kernel(x_ref, wc_ref, w1f_ref, pef_ref, w2_ref, kc_ref, vct_ref, *, nch):
    def load_xp(t, p):
        return x_ref[:, p, t * KVD:(t + 1) * KVD]
    kc, vc = _compress(load_xp, wc_ref, w1f_ref, pef_ref, w2_ref, nch)
    kc_ref[...] = kc
    vct_ref[...] = vc.T.astype(BF16)


def _compress_prompt(kvf, n_batch, seq, cw):
    nch = seq // CMP_STRIDE
    wc, w1f, pef, w2bd = cw
    return pl.pallas_call(
        functools.partial(_compress_prompt_kernel, nch=nch),
        grid=(n_batch,),
        in_specs=[pl.BlockSpec((nch, CMP_STRIDE, 2 * KVD), lambda b: (b, 0, 0)),
                  _const_spec(wc.shape), _const_spec(w1f.shape), _const_spec(pef.shape), _const_spec(w2bd.shape)],
        out_specs=[pl.BlockSpec((nch, KVD), lambda b: (b, 0)), pl.BlockSpec((None, KVD, nch), lambda b: (b, 0, 0))],
        out_shape=[jax.ShapeDtypeStruct((n_batch * nch, KVD), F32), jax.ShapeDtypeStruct((n_batch, KVD, nch), BF16)],
        compiler_params=_cparams("parallel"), name="compress_prompt")(
            kvf.reshape(n_batch * nch, CMP_STRIDE, kvf.shape[1]), wc, w1f, pef, w2bd)


NEAR_CMP = 16


def _cmp_select_kernel(qt_ref, kc_ref, vct_ref, nt_ref, ovlt_ref, gl_ref, o_ref, selt_ref, s_scr, *, nch, n_blocks):
    i = pl.program_id(1)
    gates = jax.nn.sigmoid(gl_ref[0:N_GATE, :])
    cs = i * (Q_BLOCK // CMP_STRIDE) - NEAR_CMP // 2
    start = pl.multiple_of(jnp.maximum(cs, 0), 8)
    variant = jnp.minimum(i, 1)
    far_ok = lax.broadcasted_iota(jnp.int32, (nch, Q_BLOCK), 0) < cs
    kc_all = kc_ref[...].astype(BF16)
    kc_near = kc_ref[pl.ds(start, NEAR_CMP), :].astype(BF16)
    j_io = lax.broadcasted_iota(jnp.int32, (n_blocks, Q_BLOCK), 0)
    q_io = lax.broadcasted_iota(jnp.int32, (n_blocks, Q_BLOCK), 1)
    cur = (i * Q_BLOCK + q_io) // SEL_BLOCK
    forced = (j_io == 0) | (j_io == cur) | (j_io == cur - 1)
    outs, sels = [], []
    for g in range(N_KV):
        qt = _q_group_t(qt_ref, g)
        gsl = slice(g * HEAD_DIM, (g + 1) * HEAD_DIM)
        s_far = _dot(kc_all[:, gsl], qt)
        s_scr[...] = jnp.concatenate([jnp.where(far_ok, sl, NEG) for sl in _head_slabs(s_far)], axis=1)
        s_scr[pl.ds(start, NEAR_CMP), :] = _heads_add(
            _dot(kc_near[:, gsl], qt), [nt_ref[variant, g * GROUP + r] for r in range(GROUP)])
        s = s_scr[...]
        m = jnp.max(s, axis=0, keepdims=True)
        m = jnp.where(m > 0.5 * NEG, m, 0.0)
        e = jnp.exp(s - m)
        den = jnp.sum(e, axis=0, keepdims=True)
        inv = 1.0 / jnp.where(den > 0, den, 1.0)
        outs += _head_slabs(_dot(vct_ref[gsl, :], e.astype(BF16)) * (inv * _gate_row(gates, 0, g)))
        p = e * inv
        pg = None
        for sl in _head_slabs(p):
            pg = sl if pg is None else pg + sl
        imp = _dot_01_by_f32(ovlt_ref[...], pg)
        score = jnp.where(j_io <= cur, jnp.where(forced, FORCED_SCORE, imp), NEG)
        sels.append(_topk_mask(score, min(N_SELECT, n_blocks), 0))
    o_ref[...] = jnp.concatenate(outs, axis=0).T
    selt_ref[...] = jnp.concatenate(sels, axis=0).astype(BF16)


def _overlap_np(nch, n_blocks, n_cols):
    ci = np.arange(nch)[:, None] * CMP_STRIDE
    sj = np.arange(n_cols)[None, :] * SEL_BLOCK
    ov = (ci < sj + SEL_BLOCK) & (ci + CMP_LEN > sj) & (np.arange(n_cols)[None, :] < n_blocks)
    ov[nch - 1] = False
    return ov.astype(np.float32)


def _cmp_select(qt, kc, vct, near_tiles, gate_t, n_batch, seq):
    nq = seq // Q_BLOCK
    nch = seq // CMP_STRIDE
    n_blocks = seq // SEL_BLOCK
    n = n_batch * seq
    ovlt = jnp.asarray(_overlap_np(nch, n_blocks, n_blocks).T, BF16)
    return pl.pallas_call(
        functools.partial(_cmp_select_kernel, nch=nch, n_blocks=n_blocks),
        grid=(n_batch, nq),
        in_specs=[_qblock_cols(HD, nq),
                  pl.BlockSpec((nch, KVD), lambda b, i: (b, 0)),
                  pl.BlockSpec((None, KVD, nch), lambda b, i: (b, 0, 0)),
                  _const_spec(near_tiles.shape), _const_spec(ovlt.shape), _qblock_cols(LANE, nq)],
        out_specs=[_qblock_rows(HD, nq), _qblock_cols(N_KV * n_blocks, nq)],
        out_shape=[jax.ShapeDtypeStruct((n, HD), F32), jax.ShapeDtypeStruct((N_KV * n_blocks, n), BF16)],
        scratch_shapes=[pltpu.VMEM((nch, GROUP * Q_BLOCK), F32)],
        compiler_params=_cparams("parallel", "parallel"), name="cmp_select")(
            qt, kc, vct, near_tiles, ovlt, gate_t)


FAR_CHUNK = 512


def _selected_kernel(qt_ref, k_ref, vt_ref, selt_ref, ext_ref, bt_ref, gl_ref, o_ref, *, n_blocks):
    i = pl.program_id(1)
    gates = jax.nn.sigmoid(gl_ref[0:N_GATE, :])
    far_end = (i - 1) * Q_BLOCK
    n_far = (i + 2) // (FAR_CHUNK // Q_BLOCK)
    outs = []
    for g in range(N_KV):
        qt = _q_group_t(qt_ref, g)
        gsl = slice(g * HEAD_DIM, (g + 1) * HEAD_DIM)
        sel_g = selt_ref[g * n_blocks:(g + 1) * n_blocks, :]

        def mask_bias(start, size):
            return (_dot(ext_ref[pl.ds(start, size), :], sel_g) - 1.0) * (-NEG)

        s_parts, v_parts = [], []
        for j in (1, 0):
            start = pl.multiple_of(jnp.maximum(i - j, 0) * Q_BLOCK, Q_BLOCK)
            mb = mask_bias(start, Q_BLOCK)
            if j > 0:
                mb = mb + jnp.where(i >= j, 0.0, NEG)
            sj = _dot(k_ref[pl.ds(start, Q_BLOCK), gsl], qt)
            s_parts.append(_heads_add(sj, [bt_ref[j, g * GROUP + r] + mb for r in range(GROUP)]))
            v_parts.append(vt_ref[gsl, pl.ds(start, Q_BLOCK)])
        s = jnp.concatenate(s_parts, axis=0)
        m0 = jnp.max(s, axis=0, keepdims=True)
        e = jnp.exp(s - m0)
        l0 = jnp.sum(e, axis=0, keepdims=True)
        acc0 = _dot(jnp.concatenate(v_parts, axis=1), e.astype(BF16))

        def far_step(c, carry):
            m_old, l_old, acc_old = carry
            start = pl.multiple_of(c * FAR_CHUNK, FAR_CHUNK)
            kpos = start + lax.broadcasted_iota(jnp.int32, (FAR_CHUNK, Q_BLOCK), 0)
            mb = jnp.where(kpos < far_end, mask_bias(start, FAR_CHUNK), NEG)
            sc = _heads_add(_dot(k_ref[pl.ds(start, FAR_CHUNK), gsl], qt), [mb] * GROUP)
            m_new = jnp.maximum(m_old, jnp.max(sc, axis=0, keepdims=True))
            alpha = jnp.exp(m_old - m_new)
            p = jnp.exp(sc - m_new)
            l_new = alpha * l_old + jnp.sum(p, axis=0, keepdims=True)
            acc_new = alpha * acc_old + _dot(vt_ref[gsl, pl.ds(start, FAR_CHUNK)], p.astype(BF16))
            return m_new, l_new, acc_new

        _, den, acc = lax.fori_loop(0, n_far, far_step, (m0, l0, acc0))
        outs += _head_slabs(acc * (_gate_row(gates, 1, g) / den))
    o_ref[...] = jnp.concatenate(outs, axis=0).T


def _selected(qt, kvb, vt, v_idx, selt, near_tiles, gate_t, n_batch, seq):
    assert seq % FAR_CHUNK == 0
    nq = seq // Q_BLOCK
    n_blocks = seq // SEL_BLOCK
    ext = jnp.asarray((np.arange(seq)[:, None] // SEL_BLOCK == np.arange(n_blocks)[None, :]).astype(np.float32), BF16)
    return pl.pallas_call(
        functools.partial(_selected_kernel, n_blocks=n_blocks),
        grid=(n_batch, nq),
        in_specs=[_qblock_cols(HD, nq),
                  pl.BlockSpec((seq, KVD), lambda b, i: (b, 2)),
                  pl.BlockSpec((KVD, seq), lambda b, i: (v_idx, b)),
                  _qblock_cols(N_KV * n_blocks, nq), _const_spec(ext.shape), _const_spec(near_tiles.shape),
                  _qblock_cols(LANE, nq)],
        out_specs=_qblock_rows(HD, nq),
        out_shape=jax.ShapeDtypeStruct((n_batch * seq, HD), F32),
        compiler_params=_cparams("parallel", "parallel"), name="selected_attn")(
            qt, kvb, vt, selt, ext, near_tiles, gate_t)


def _block_diag_q(q):
    z = jnp.zeros_like(q)
    row = lax.broadcasted_iota(jnp.int32, (N_HEADS, 2 * HEAD_DIM), 0)
    return jnp.where(row < GROUP, jnp.concatenate([q, z], axis=1), jnp.concatenate([z, q], axis=1))


def _pick_group_half(o_full):
    row = lax.broadcasted_iota(jnp.int32, (N_HEADS, HEAD_DIM), 0)
    return jnp.where(row < GROUP, o_full[:, :HEAD_DIM], o_full[:, HEAD_DIM:])


def _window_decode_kernel(*refs, n_samp, use_sink, use_gate):
    it = iter(refs)
    q_ref, nkv_ref, st_ref, br_ref, b0_ref = next(it), next(it), next(it), next(it), next(it)
    sink_ref = next(it) if use_sink else None
    gl_ref = next(it) if use_gate else None
    o_ref = next(it)
    for s_i in range(n_samp):
        qbd = _block_diag_q(q_ref[s_i])
        kt = st_ref[s_i, 0, 0].astype(BF16)
        vt = st_ref[s_i, 0, 1].astype(BF16)
        new = nkv_ref[s_i]
        s_buf = _dot(qbd, kt) + br_ref[...]
        s_new = jnp.sum(qbd.astype(F32) * new[:, 0:KVD], axis=1, keepdims=True) + b0_ref[...]
        m = jnp.maximum(jnp.max(s_buf, axis=1, keepdims=True), s_new)
        if use_sink:
            m = jnp.maximum(m, sink_ref[...])
        e_buf = jnp.exp(s_buf - m)
        e_new = jnp.exp(s_new - m)
        den = jnp.sum(e_buf, axis=1, keepdims=True) + e_new
        if use_sink:
            den = den + jnp.exp(sink_ref[...] - m)
        o_full = _dot_nt(e_buf.astype(BF16), vt) + e_new * new[:, KVD:2 * KVD]
        f = 1.0 / den
        if use_gate:
            f = f * jax.nn.sigmoid(gl_ref[s_i])
        o_ref[s_i] = _pick_group_half(o_full) * f


def _window_decode(q, new_kv, state, layer, bias_row, bias0, sinks, gate_logits):
    db, lb = state.shape[0], state.shape[-1]
    n_samp = 8 if db % 8 == 0 else 1
    use_sink = sinks is not None
    use_gate = gate_logits is not None
    ins = [q, new_kv.reshape(db, 1, 2 * KVD), state, bias_row, bias0]
    in_specs = [pl.BlockSpec((n_samp, N_HEADS, HEAD_DIM), lambda t: (t, 0, 0)),
                pl.BlockSpec((n_samp, 1, 2 * KVD), lambda t: (t, 0, 0)),
                pl.BlockSpec((n_samp, 1, 2, KVD, lb), lambda t: (t, layer, 0, 0, 0)),
                _const_spec(bias_row.shape), _const_spec(bias0.shape)]
    if use_sink:
        ins.append(sinks.astype(F32).reshape(N_HEADS, 1))
        in_specs.append(_const_spec((N_HEADS, 1)))
    if use_gate:
        ins.append(gate_logits)
        in_specs.append(pl.BlockSpec((n_samp, N_HEADS, 1), lambda t: (t, 0, 0)))
    return pl.pallas_call(
        functools.partial(_window_decode_kernel, n_samp=n_samp, use_sink=use_sink, use_gate=use_gate),
        grid=(db // n_samp,), in_specs=in_specs,
        out_specs=pl.BlockSpec((n_samp, N_HEADS, HEAD_DIM), lambda t: (t, 0, 0)),
        out_shape=jax.ShapeDtypeStruct((db, N_HEADS, HEAD_DIM), F32),
        compiler_params=_cparams("parallel"), name="window_decode")(*ins)


N_KIND = 4
XT_ROWS = 1024


def _nsa_decode_kernel(pt_ref, q_ref, nkv_ref, gl_ref, pool_ref, wc_ref, w1f_ref, pef_ref, w2_ref,
                       cb_ref, sb_ref, b0_ref, ovl_ref, ex_ref, oc_ref, os_ref,
                       buf_a, buf_b, xrow, sem, *, layer, n_pages, past, n_blocks, nbp):
    step = pl.program_id(0)
    n_steps = pl.num_programs(0)
    nch = past // CMP_STRIDE
    n_past = past // SEL_BLOCK

    def page_copy(page, j, buf, slot):
        return pltpu.make_async_copy(
            pool_ref.at[page, layer], buf.at[:, :, pl.ds(j * PAGE_SIZE, PAGE_SIZE)], sem.at[slot])

    def start_fetch(sample, buf, slot):
        def body(j, c):
            page_copy(pt_ref[sample * n_pages + j], j, buf, slot).start()
            return c
        lax.fori_loop(0, n_pages, body, 0)

    def wait_fetch(buf, slot):
        def body(j, c):
            page_copy(0, j, buf, slot).wait()
            return c
        lax.fori_loop(0, n_pages, body, 0)

    def compute(buf, s_i):
        qbd = _block_diag_q(q_ref[s_i])
        new = nkv_ref[s_i]
        gates = jax.nn.sigmoid(gl_ref[s_i])

        def to_rows(t):
            xr = min(XT_ROWS, past)
            for c in range(past // xr):
                xrow[t, c * xr:(c + 1) * xr, :] = buf[t, :, c * xr:(c + 1) * xr].T

        def load_xp(t, p):
            return xrow[t, pl.ds(p, nch, stride=CMP_STRIDE), :]
        kc, vc = _compress(load_xp, wc_ref, w1f_ref, pef_ref, w2_ref, nch, prepare=to_rows)
        s = _dot_nt(qbd, kc.astype(BF16)) + cb_ref[...]
        m = jnp.max(s, axis=1, keepdims=True)
        m = jnp.where(m > 0.5 * NEG, m, 0.0)
        e = jnp.exp(s - m)
        den = jnp.sum(e, axis=1, keepdims=True)
        inv = 1.0 / jnp.where(den > 0, den, 1.0)
        oc_full = _dot(e.astype(BF16), vc.astype(BF16))
        oc_ref[s_i] = _pick_group_half(oc_full) * (inv * gates[0])
        p = e * inv
        j_io = lax.broadcasted_iota(jnp.int32, (1, nbp), 1)
        cur = past // SEL_BLOCK
        forced = (j_io == 0) | (j_io == cur) | (j_io == cur - 1)
        sel_rows = []
        for g in range(N_KV):
            pg = jnp.sum(p[g * GROUP:(g + 1) * GROUP], axis=0, keepdims=True)
            imp = _dot_f32_by_01(pg, ovl_ref[...])
            score = jnp.where(j_io <= cur, jnp.where(forced, FORCED_SCORE, imp), NEG)
            sel_rows.append(_topk_mask_row(score, min(N_SELECT, n_blocks))[:, 0:n_past])
        row = lax.broadcasted_iota(jnp.int32, (N_HEADS, n_past), 0)
        sel16 = jnp.where(row < GROUP, sel_rows[0], sel_rows[1]).astype(BF16)
        selm = _dot(sel16, ex_ref[...])
        s2 = _dot(qbd, buf[2].astype(BF16)) + sb_ref[...] + (selm - 1.0) * (-NEG)
        s_new = jnp.sum(qbd.astype(F32) * new[:, 0:KVD], axis=1, keepdims=True) + b0_ref[...]
        m2 = jnp.maximum(jnp.max(s2, axis=1, keepdims=True), s_new)
        e2 = jnp.exp(s2 - m2)
        e_new = jnp.exp(s_new - m2)
        den2 = jnp.sum(e2, axis=1, keepdims=True) + e_new
        os_full = _dot_nt(e2.astype(BF16), buf[3].astype(BF16)) + e_new * new[:, KVD:2 * KVD]
        os_ref[s_i] = _pick_group_half(os_full) * (gates[1] / den2)

    @pl.when(step == 0)
    def _():
        start_fetch(0, buf_a, 0)

    start_fetch(2 * step + 1, buf_b, 1)
    wait_fetch(buf_a, 0)
    compute(buf_a, 0)

    @pl.when(step + 1 < n_steps)
    def _():
        start_fetch(2 * step + 2, buf_a, 0)

    wait_fetch(buf_b, 1)
    compute(buf_b, 1)


def _nsa_decode(q, new_sel, gate_logits2, pool, page_table, layer, cw, cmp_bias, sel_bias, bias0):
    db, n_pages = page_table.shape
    assert db % 2 == 0
    past = n_pages * PAGE_SIZE
    nch = past // CMP_STRIDE
    n_past = past // SEL_BLOCK
    n_blocks = n_past + 1
    nbp = -(-n_blocks // LANE) * LANE
    wc, w1f, pef, w2bd = cw
    ovl = jnp.asarray(_overlap_np(nch, n_blocks, nbp), BF16)
    ex = jnp.asarray((np.arange(n_past)[:, None] == (np.arange(past)[None, :] // SEL_BLOCK)).astype(np.float32), BF16)
    consts = [wc, w1f, pef, w2bd, cmp_bias, sel_bias, bias0, ovl, ex]
    per2 = lambda *tail: pl.BlockSpec((2,) + tail, lambda t, pt: (t,) + (0,) * len(tail))
    grid_spec = pltpu.PrefetchScalarGridSpec(
        num_scalar_prefetch=1, grid=(db // 2,),
        in_specs=[per2(N_HEADS, HEAD_DIM), per2(1, 2 * KVD), per2(2, N_HEADS, 1),
                  pl.BlockSpec(memory_space=pl.ANY)] + [_const_spec(c.shape) for c in consts],
        out_specs=[per2(N_HEADS, HEAD_DIM), per2(N_HEADS, HEAD_DIM)],
        scratch_shapes=[pltpu.VMEM((N_KIND, KVD, past), F32), pltpu.VMEM((N_KIND, KVD, past), F32),
                        pltpu.VMEM((2, past, KVD), F32), pltpu.SemaphoreType.DMA((2,))])
    out = jax.ShapeDtypeStruct((db, N_HEADS, HEAD_DIM), F32)
    return pl.pallas_call(
        functools.partial(_nsa_decode_kernel, layer=layer, n_pages=n_pages, past=past,
                          n_blocks=n_blocks, nbp=nbp),
        grid_spec=grid_spec, out_shape=[out, out],
        compiler_params=_cparams("arbitrary"), name="nsa_decode")(
            page_table.reshape(-1), q, new_sel.reshape(db, 1, 2 * KVD), gate_logits2, pool, *consts)


def _shift_kernel(st_ref, new_ref, o_ref, *, lb):
    x = st_ref[...]
    lane = lax.broadcasted_iota(jnp.int32, x.shape, 2)
    o_ref[...] = jnp.where(lane == lb - 1, new_ref[...], pltpu.roll(x, lb - 1, 2))


def _shift_state(state, new_rows):
    db, w, lb = state.shape
    ns = max(1, min(db, (4 << 20) // (w * lb * 4)))
    assert db % ns == 0
    blk = pl.BlockSpec((ns, w, lb), lambda t: (t, 0, 0))
    return pl.pallas_call(
        functools.partial(_shift_kernel, lb=lb), grid=(db // ns,),
        in_specs=[blk, pl.BlockSpec((ns, w, 1), lambda t: (t, 0, 0))], out_specs=blk,
        out_shape=jax.ShapeDtypeStruct((db, w, lb), state.dtype),
        compiler_params=_cparams("parallel"), name="shift_state")(state, new_rows.reshape(db, w, 1))


def _decode_bias_row(rel_bias, past, kpos, window):
    dist = past - kpos
    valid = (dist >= 0) if window is None else ((dist >= 0) & (dist < window))
    return _bias_tile(rel_bias, dist[None, :], valid[None, :], False)[:, 0, :]


def kernel(x_prompt, x_sample, state_swa, cache_nsa, state_nsa_win, page_table, rel_bias, norm_mix, norm_mlp,
           norm_final, w_in_a, b_in_a, w_out_a, b_out_a, sinks_a, w_in_b, w_out_b, w_cmp1, w_cmp2, pe_cmp,
           w_up, w_down):
    n_batch, seq, _ = x_prompt.shape
    db = x_sample.shape[0]
    assert x_sample.shape[1] == 1
    depth = norm_mix.shape[0]
    n_la, n_lb = w_in_a.shape[0], w_in_b.shape[0]
    lba, lbb = state_swa.shape[1], state_nsa_win.shape[1]
    n_pages = page_table.shape[1]
    past = n_pages * PAGE_SIZE
    assert seq >= WIN_B and lba == WIN_A and lbb == WIN_B and past >= WIN_B

    kk = np.arange(Q_BLOCK)[:, None]
    qq = np.arange(Q_BLOCK)[None, :]
    d0, d1 = qq - kk, Q_BLOCK + qq - kk
    swa_tiles = jnp.stack([_bias_tile(rel_bias, d0, (d0 >= 0) & (d0 < WIN_A), False),
                           _bias_tile(rel_bias, d1, (d1 >= 0) & (d1 < WIN_A), False)])
    near_tiles = jnp.stack([_bias_tile(rel_bias, d0, d0 >= 0, True),
                            _bias_tile(rel_bias, d1, d1 >= 0, True)])
    nback_b = WIN_B // Q_BLOCK
    far_mask = jnp.asarray(np.where(kk > qq, 0.0, NEG), F32)
    no_mask = jnp.zeros((8, LANE), F32)
    cc = np.arange(NEAR_CMP)[:, None]
    dc0 = qq - CMP_STRIDE * cc - (CMP_LEN - 1)
    dc1 = qq - CMP_STRIDE * (cc - NEAR_CMP // 2) - (CMP_LEN - 1)
    cmp_tiles = jnp.stack([_bias_tile(rel_bias, dc0, dc0 >= 0, True), _bias_tile(rel_bias, dc1, dc1 >= 0, True)])
    nch_d = past // CMP_STRIDE
    cmp_row = _decode_bias_row(rel_bias, past, np.arange(nch_d) * CMP_STRIDE + CMP_LEN - 1, None)
    sel_row = _decode_bias_row(rel_bias, past, np.arange(past), None)
    swa_row = _decode_bias_row(rel_bias, past, past - lba + np.arange(lba), WIN_A)
    win_row = _decode_bias_row(rel_bias, past, past - lbb + np.arange(lbb), WIN_B)
    bias0 = rel_bias.astype(F32)[0].reshape(N_HEADS, 1)

    xp = x_prompt.reshape(n_batch * seq, D_MODEL)
    xs = x_sample.reshape(db, D_MODEL)
    to_minor = lambda a: jnp.transpose(a, (0, 2, 3, 4, 5, 1))
    st_swa = to_minor(state_swa).reshape(db, n_la, 2, KVD, lba)
    st_win = to_minor(state_nsa_win).reshape(db, n_lb, 2, KVD, lbb)
    pool = to_minor(cache_nsa).reshape(cache_nsa.shape[0], n_lb, N_KIND, KVD, PAGE_SIZE)
    swa_p, swa_new, rows_p, rows_s, win_p, win_new = [], [], [], [], [], []

    for layer in range(depth):
        final = norm_final if layer == depth - 1 else None
        if layer % 2 == 0:
            a = layer // 2
            qt_p, kv_p, kvb_p, vt_p = _project(xp, norm_mix[layer], w_in_a[a], b_in_a[a], 2 * KVD, False, (KVD,))
            q_s, kv_s, _ = _project(xs, norm_mix[layer], w_in_a[a], b_in_a[a], 2 * KVD, False, None)
            o_p = _banded(qt_p, kvb_p, 0, vt_p, 0, n_batch, seq, ("bias", "bias"), swa_tiles, no_mask,
                          sinks_a[a], None, None)
            o_s = _window_decode(jnp.swapaxes(q_s, 0, 1), kv_s, st_swa, a, swa_row, bias0, sinks_a[a], None)
            swa_p.append(kv_p.reshape(n_batch, seq, 2, N_KV, HEAD_DIM)[:, seq - WIN_A:])
            swa_new.append(kv_s)
            xp = _out_mlp(xp, [o_p], w_out_a[a], b_out_a[a], norm_mlp[layer], w_up[layer], w_down[layer], final)
            xs = _out_mlp(xs, [o_s.reshape(db, HD)], w_out_a[a], b_out_a[a], norm_mlp[layer], w_up[layer],
                          w_down[layer], final)
        else:
            b = layer // 2
            cw = _compress_weights(w_cmp1[b], w_cmp2[b], pe_cmp[b])
            w_in = jnp.pad(w_in_b[b], ((0, 0), (0, LANE - N_GATE)))
            qt_p, kv_p, kvb_p, vt_p, glt_p = _project(xp, norm_mix[layer], w_in, None, 6 * KVD, True,
                                                      (3 * KVD, 5 * KVD))
            q_s, kv_s, _, gl_s = _project(xs, norm_mix[layer], w_in, None, 6 * KVD, True, None)
            kc, vct = _compress_prompt(kv_p, n_batch, seq, cw)
            o_c, selt = _cmp_select(qt_p, kc, vct, cmp_tiles, glt_p, n_batch, seq)
            o_sel = _selected(qt_p, kvb_p, vt_p, 0, selt, near_tiles, glt_p, n_batch, seq)
            kinds = ("bias", "bias") + ("none",) * (nback_b - 2) + ("mask",)
            o_w = _banded(qt_p, kvb_p, 4, vt_p, 1, n_batch, seq, kinds, near_tiles, far_mask, None, glt_p, 2)
            rows_p.append(kv_p[:, :4 * KVD].reshape(n_batch, seq, 4, N_KV, HEAD_DIM))
            win_p.append(kv_p.reshape(n_batch, seq, 6 * KVD)[:, seq - WIN_B:, 4 * KVD:]
                         .reshape(n_batch, WIN_B, 2, N_KV, HEAD_DIM))
            xp = _out_mlp(xp, [o_c, o_sel, o_w], w_out_b[b], None, norm_mlp[layer], w_up[layer], w_down[layer], final)
            q_sd = jnp.swapaxes(q_s, 0, 1)
            gl_s3 = gl_s.reshape(db, 3, N_HEADS, 1)
            oc_s, os_s = _nsa_decode(q_sd, kv_s[:, 2 * KVD:4 * KVD], gl_s3[:, 0:2], pool, page_table, b, cw,
                                     cmp_row, sel_row, bias0)
            ow_s = _window_decode(q_sd, kv_s[:, 4 * KVD:], st_win, b, win_row, bias0, None, gl_s3[:, 2])
            rows_s.append(kv_s[:, :4 * KVD].reshape(db, 1, 4, N_KV, HEAD_DIM))
            win_new.append(kv_s[:, 4 * KVD:])
            xs = _out_mlp(xs, [oc_s.reshape(db, HD), os_s.reshape(db, HD), ow_s.reshape(db, HD)], w_out_b[b], None,
                          norm_mlp[layer], w_up[layer], w_down[layer], final)

    y_prompt = xp.reshape(n_batch, seq, D_MODEL)
    y_sample = xs.reshape(db, 1, D_MODEL)
    from_minor = lambda a, nl, lb: jnp.transpose(a.reshape(db, nl, 2, N_KV, HEAD_DIM, lb), (0, 5, 1, 2, 3, 4))
    swa_sample = _shift_state(st_swa.reshape(db, n_la * 2 * KVD, lba), jnp.concatenate(swa_new, axis=1))
    win_sample = _shift_state(st_win.reshape(db, n_lb * 2 * KVD, lbb), jnp.concatenate(win_new, axis=1))
    return (y_prompt, y_sample,
            jnp.stack(swa_p, axis=2),
            from_minor(swa_sample, n_la, lba),
            jnp.stack(rows_p, axis=2), jnp.stack(rows_s, axis=2),
            jnp.stack(win_p, axis=2),
            from_minor(win_sample, n_lb, lbb))
```

```python
import functools
import math

import numpy as np
import jax
import jax.numpy as jnp
from jax import lax
from jax.experimental import pallas as pl
from jax.experimental.pallas import tpu as pltpu

D_MODEL = 1024
N_HEADS = 16
HEAD_DIM = 64
N_KV = 2
GROUP = N_HEADS // N_KV
HD = N_HEADS * HEAD_DIM
KVD = N_KV * HEAD_DIM
D_FF = 4 * D_MODEL
PAGE_SIZE = 128
WIN_A = 128
WIN_B = 512
Q_BLOCK = 128
CMP_STRIDE = 16
CMP_LEN = 2 * CMP_STRIDE
CMP_HIDDEN = 128
SEL_BLOCK = 64
N_SELECT = 16
FORCED_SCORE = 1e9
N_BUCKETS = 32
MAX_DISTANCE = 128
NORM_EPS = 1e-5
SCALE = HEAD_DIM ** -0.5
LOG2E = math.log2(math.e)
N_GATE = 3 * N_HEADS
LANE = 128

NEG = -1e30
REMOVED = -2e30
F32 = jnp.float32
BF16 = jnp.bfloat16
VMEM_LIMIT = 56 * 1024 * 1024


def _cparams(*sem):
    return pltpu.CompilerParams(dimension_semantics=sem, vmem_limit_bytes=VMEM_LIMIT)


def _const_spec(shape):
    nd = len(shape)
    return pl.BlockSpec(shape, lambda *_: (0,) * nd, pipeline_mode=pl.Buffered(1))


def _bucket_np(dist):
    n = np.maximum(dist, 0)
    exact = N_BUCKETS // 2
    nf = np.maximum(n, exact).astype(np.float32)
    large = exact + (np.log(nf / np.float32(exact)) / np.float32(math.log(MAX_DISTANCE / exact))
                     * np.float32(N_BUCKETS - exact)).astype(np.int32)
    return np.where(n < exact, n, np.minimum(large, N_BUCKETS - 1)).astype(np.int32)


def _bias_tile(rel_bias, dist, valid, shift):
    tb = rel_bias.astype(F32)
    if shift:
        tb = tb - tb[N_BUCKETS - 1][None, :]
    bucket = jnp.asarray(_bucket_np(dist).reshape(1, -1))
    onehot = (bucket == jnp.arange(N_BUCKETS, dtype=jnp.int32)[:, None]).astype(F32)
    b = jnp.dot(tb.T, onehot, precision=lax.Precision.HIGHEST).reshape((N_HEADS,) + dist.shape)
    return jnp.where(jnp.asarray(valid)[None], b, NEG)


def _rms(x, g):
    ms = jnp.mean(x * x, axis=-1, keepdims=True)
    return x * lax.rsqrt(ms + NORM_EPS) * g


def _dot(a, b):
    return jnp.dot(a, b, preferred_element_type=F32)


def _dot_nt(a, b):
    return lax.dot_general(a, b, (((1,), (1,)), ((), ())), preferred_element_type=F32)


def _split3(a):
    hi = a.astype(BF16)
    r1 = a - hi.astype(F32)
    mid = r1.astype(BF16)
    lo = (r1 - mid.astype(F32)).astype(BF16)
    return hi, mid, lo


def _dot_f32_by_01(a, b01):
    hi, mid, lo = _split3(a)
    return _dot(hi, b01) + _dot(mid, b01) + _dot(lo, b01)


def _dot_01_by_f32(b01, a):
    hi, mid, lo = _split3(a)
    return _dot(b01, hi) + _dot(b01, mid) + _dot(b01, lo)


def _topk_mask(score, k, axis):
    n = score.shape[axis]
    iota = lax.broadcasted_iota(jnp.int32, score.shape, axis).astype(F32)
    sel = jnp.zeros_like(score)
    s = score
    for _ in range(k):
        m = jnp.max(s, axis=axis, keepdims=True)
        first = jnp.min(jnp.where(s == m, iota, float(n)), axis=axis, keepdims=True)
        hit = iota == first
        ok = jnp.where(m > 0.5 * NEG, 1.0, 0.0)
        sel = jnp.where(hit, ok, sel)
        s = jnp.where(hit, REMOVED, s)
    return sel


def _topk_mask_row(score, k):
    n = score.shape[1]
    s_j = jnp.broadcast_to(score, (n, n))
    s_i = jnp.concatenate([jnp.broadcast_to(score, (LANE, n)).T] * (n // LANE), axis=1)
    i_io = lax.broadcasted_iota(jnp.int32, (n, n), 0)
    j_io = lax.broadcasted_iota(jnp.int32, (n, n), 1)
    ahead = jnp.where(s_i > s_j, 1.0, jnp.where(s_i == s_j, jnp.where(i_io < j_io, 1.0, 0.0), 0.0))
    rank = jnp.sum(ahead, axis=0, keepdims=True)
    return jnp.where(rank < k, jnp.where(score > 0.5 * NEG, 1.0, 0.0), 0.0)


def _q_group_t(qt_ref, g):
    return jnp.concatenate(
        [qt_ref[(g * GROUP + r) * HEAD_DIM:(g * GROUP + r + 1) * HEAD_DIM, :] for r in range(GROUP)], axis=1)


def _heads_add(s, tiles):
    return jnp.concatenate([s[:, r * Q_BLOCK:(r + 1) * Q_BLOCK] + tiles[r] for r in range(GROUP)], axis=1)


def _head_row(rows):
    return jnp.concatenate(rows, axis=1)


def _head_slabs(acc_t):
    return [acc_t[:, r * Q_BLOCK:(r + 1) * Q_BLOCK] for r in range(GROUP)]


def _proj_kernel(*refs, n_kv, has_bias, has_gate, v_cols, n_rows):
    it = iter(refs)
    x_ref, g_ref, w_ref = next(it), next(it), next(it)
    b_ref = next(it) if has_bias else None
    if n_rows:
        next(it)
    q_ref, kv_ref, kvb_ref = next(it), next(it), next(it)
    vt_ref = next(it) if v_cols is not None else None
    gl_ref = next(it) if has_gate else None
    rows_ref = next(it) if n_rows else None
    xn = _rms(x_ref[...], g_ref[...]).astype(BF16)
    u = _dot(xn, w_ref[...])
    if has_bias:
        u = u + b_ref[...]
    kv = u[:, HD:HD + n_kv]
    kv_ref[...] = kv
    kvb_ref[...] = kv.astype(BF16)
    if v_cols is None:
        for h in range(N_HEADS):
            q_ref[h] = (u[:, h * HEAD_DIM:(h + 1) * HEAD_DIM] * SCALE).astype(BF16)
        if has_gate:
            gl_ref[...] = u[:, HD + n_kv:HD + n_kv + N_GATE]
    else:
        q_ref[...] = (u[:, :HD] * (SCALE * LOG2E)).T.astype(BF16)
        for idx, c0 in enumerate(v_cols):
            vt_ref[idx * KVD:(idx + 1) * KVD, :] = u[:, HD + c0:HD + c0 + KVD].T.astype(BF16)
        if has_gate:
            gl_ref[...] = u[:, HD + n_kv:HD + n_kv + LANE].T
        if n_rows:
            rows_ref[...] = u[:, HD:HD + n_rows].T


def _project(x, g, w, b, n_kv, has_gate, v_cols, rows_into=None):
    n = x.shape[0]
    dout = w.shape[1]
    tm = min(512, n)
    assert n % tm == 0
    has_bias = b is not None
    prompt = v_cols is not None
    ins = [x, g.reshape(1, D_MODEL), w.astype(BF16)]
    in_specs = [pl.BlockSpec((tm, D_MODEL), lambda t: (t, 0)),
                _const_spec((1, D_MODEL)), _const_spec((D_MODEL, dout))]
    if has_bias:
        ins.append(b.reshape(1, dout))
        in_specs.append(_const_spec((1, dout)))
    aliases = {}
    n_rows = 0
    if rows_into is not None:
        rows_arr, rows_layer = rows_into
        n_rows, seq = rows_arr.shape[2], rows_arr.shape[3]
        assert seq % tm == 0
        aliases = {len(ins): 0}
        ins.append(rows_arr)
        in_specs.append(pl.BlockSpec(memory_space=pl.ANY))
    rows = lambda wd, dt: (jax.ShapeDtypeStruct((n, wd), dt), pl.BlockSpec((tm, wd), lambda t: (t, 0)))
    cols = lambda ht, dt: (jax.ShapeDtypeStruct((ht, n), dt), pl.BlockSpec((ht, tm), lambda t: (0, t)))
    if prompt:
        outs = [cols(HD, BF16), rows(n_kv, F32), rows(n_kv, BF16), cols(len(v_cols) * KVD, BF16)]
        if has_gate:
            outs.append(cols(LANE, F32))
    else:
        outs = [(jax.ShapeDtypeStruct((N_HEADS, n, HEAD_DIM), BF16),
                 pl.BlockSpec((N_HEADS, tm, HEAD_DIM), lambda t: (0, t, 0))),
                rows(n_kv, F32), rows(n_kv, BF16)]
        if has_gate:
            outs.append(rows(N_GATE, F32))
    if n_rows:
        spb = seq // tm
        aliases = {k: len(outs) for k in aliases}
        outs.append((jax.ShapeDtypeStruct(rows_arr.shape, rows_arr.dtype),
                     pl.BlockSpec((None, None, n_rows, tm), lambda t: (t // spb, rows_layer, 0, t % spb))))
    return pl.pallas_call(
        functools.partial(_proj_kernel, n_kv=n_kv, has_bias=has_bias, has_gate=has_gate, v_cols=v_cols,
                          n_rows=n_rows),
        grid=(n // tm,), in_specs=in_specs, out_specs=[o[1] for o in outs], out_shape=[o[0] for o in outs],
        input_output_aliases=aliases,
        compiler_params=_cparams("parallel"), name="norm_proj")(*ins)


FF_CHUNK = 1024


def _out_mlp_kernel(*refs, n_o, has_bias, final):
    it = iter(refs)
    x_ref = next(it)
    o_refs = [next(it) for _ in range(n_o)]
    wo_ref = next(it)
    bo_ref = next(it) if has_bias else None
    gm_ref, wup_ref, wdn_ref = next(it), next(it), next(it)
    gf_ref = next(it) if final else None
    y_ref = next(it)
    o = o_refs[0][...]
    for r in o_refs[1:]:
        o = o + r[...]
    x1 = x_ref[...] + _dot(o.astype(BF16), wo_ref[...])
    if has_bias:
        x1 = x1 + bo_ref[...]
    xn = _rms(x1, gm_ref[...]).astype(BF16)
    acc = x1
    for c in range(D_FF // FF_CHUNK):
        h = jnp.maximum(_dot(xn, wup_ref[:, c * FF_CHUNK:(c + 1) * FF_CHUNK]), 0.0)
        acc = acc + _dot((h * h).astype(BF16), wdn_ref[c * FF_CHUNK:(c + 1) * FF_CHUNK, :])
    y_ref[...] = _rms(acc, gf_ref[...]) if final else acc


def _out_mlp(x, os_, w_out, b_out, g_mlp, w_up, w_down, g_final):
    n = x.shape[0]
    tm = min(256, n)
    assert n % tm == 0
    has_bias = b_out is not None
    final = g_final is not None
    row = pl.BlockSpec((tm, D_MODEL), lambda t: (t, 0))
    ins = [x, *os_, w_out.astype(BF16)]
    in_specs = [row] + [row] * len(os_) + [_const_spec((HD, D_MODEL))]
    if has_bias:
        ins.append(b_out.reshape(1, D_MODEL))
        in_specs.append(_const_spec((1, D_MODEL)))
    ins += [g_mlp.reshape(1, D_MODEL), w_up.astype(BF16), w_down.astype(BF16)]
    in_specs += [_const_spec((1, D_MODEL)), _const_spec((D_MODEL, D_FF)), _const_spec((D_FF, D_MODEL))]
    if final:
        ins.append(g_final.reshape(1, D_MODEL))
        in_specs.append(_const_spec((1, D_MODEL)))
    return pl.pallas_call(
        functools.partial(_out_mlp_kernel, n_o=len(os_), has_bias=has_bias, final=final),
        grid=(n // tm,), in_specs=in_specs, out_specs=row,
        out_shape=jax.ShapeDtypeStruct((n, D_MODEL), F32),
        compiler_params=_cparams("parallel"), name="out_mlp")(*ins)


def _qblock_cols(height, nq):
    return pl.BlockSpec((height, Q_BLOCK), lambda b, i: (0, b * nq + i))


def _qblock_rows(width, nq):
    return pl.BlockSpec((Q_BLOCK, width), lambda b, i: (b * nq + i, 0))


def _gate_row(gates, branch, g):
    return _head_row([gates[branch * N_HEADS + g * GROUP + r:branch * N_HEADS + g * GROUP + r + 1, :]
                      for r in range(GROUP)])


def _banded_kernel(*refs, kinds, use_sink, gate_branch):
    it = iter(refs)
    qt_ref, k_ref, vt_ref, bt_ref, mt_ref = next(it), next(it), next(it), next(it), next(it)
    sink_ref = next(it) if use_sink else None
    gl_ref = next(it) if gate_branch is not None else None
    o_ref = next(it)
    i = pl.program_id(1)
    gates = jax.nn.sigmoid(gl_ref[0:N_GATE, :]) if gate_branch is not None else None
    bias_slot = {}
    for j, kind in enumerate(kinds):
        if kind == "bias":
            bias_slot[j] = len(bias_slot)
    outs = []
    for g in range(N_KV):
        qt = _q_group_t(qt_ref, g)
        gsl = slice(g * HEAD_DIM, (g + 1) * HEAD_DIM)
        s_parts, v_parts = [], []
        for j, kind in enumerate(kinds):
            start = pl.multiple_of(jnp.maximum(i - j, 0) * Q_BLOCK, Q_BLOCK)
            v_parts.append(vt_ref[gsl, pl.ds(start, Q_BLOCK)])
            sj = _dot(k_ref[pl.ds(start, Q_BLOCK), gsl], qt)
            pen = jnp.where(i >= j, 0.0, NEG) if j > 0 else 0.0
            if kind == "bias":
                sj = _heads_add(sj, [bt_ref[bias_slot[j], g * GROUP + r] + pen for r in range(GROUP)])
            elif kind == "mask":
                sj = _heads_add(sj, [mt_ref[...] + pen] * GROUP)
            else:
                sj = sj + pen
            s_parts.append(sj)
        s = jnp.concatenate(s_parts, axis=0)
        vt = jnp.concatenate(v_parts, axis=1)
        m = jnp.max(s, axis=0, keepdims=True)
        if use_sink:
            sk = _head_row([jnp.full((1, Q_BLOCK), sink_ref[g * GROUP + r] * LOG2E, F32) for r in range(GROUP)])
            m = jnp.maximum(m, sk)
        e = jnp.exp2(s - m)
        den = jnp.sum(e, axis=0, keepdims=True)
        if use_sink:
            den = den + jnp.exp2(sk - m)
        f = 1.0 / den
        if gates is not None:
            f = f * _gate_row(gates, gate_branch, g)
        outs += _head_slabs(_dot(vt, e.astype(BF16)) * f)
    o_ref[...] = jnp.concatenate(outs, axis=0).T


def _banded(qt, kvb, k_col, vt, v_idx, n_batch, seq, kinds, bias_tiles, mask_tile, sinks, gate_t, gate_branch):
    nq = seq // Q_BLOCK
    use_sink = sinks is not None
    ins = [qt, kvb, vt, bias_tiles, mask_tile]
    in_specs = [_qblock_cols(HD, nq),
                pl.BlockSpec((seq, KVD), lambda b, i: (b, k_col)),
                pl.BlockSpec((KVD, seq), lambda b, i: (v_idx, b)),
                _const_spec(bias_tiles.shape), _const_spec(mask_tile.shape)]
    if use_sink:
        ins.append(sinks.astype(F32))
        in_specs.append(pl.BlockSpec(memory_space=pltpu.SMEM))
    if gate_t is not None:
        ins.append(gate_t)
        in_specs.append(_qblock_cols(LANE, nq))
    else:
        gate_branch = None
    return pl.pallas_call(
        functools.partial(_banded_kernel, kinds=kinds, use_sink=use_sink, gate_branch=gate_branch),
        grid=(n_batch, nq), in_specs=in_specs, out_specs=_qblock_rows(HD, nq),
        out_shape=jax.ShapeDtypeStruct((n_batch * seq, HD), F32),
        compiler_params=_cparams("parallel", "parallel"), name="banded_attn")(*ins)


def _compress(load_xp, wc_ref, w1f_ref, pef_ref, w2_ref, nch, prepare=None):
    outs = []
    for t in range(2):
        if prepare is not None:
            prepare(t)
        xs = jnp.concatenate([load_xp(t, p).astype(BF16) for p in range(CMP_STRIDE)], axis=1)
        acc = _dot(xs, wc_ref[t])
        pet = _dot(pef_ref[t], w1f_ref[t])
        lo = acc[:, :2 * CMP_HIDDEN]
        hi_next = pltpu.roll(acc[:, 2 * CMP_HIDDEN:], nch - 1, 0)
        hh = lo + hi_next + jnp.concatenate([pet, pet], axis=1)
        a = (hh * jax.nn.sigmoid(hh)).astype(BF16)
        outs.append(_dot(a, w2_ref[t]))
    return outs


def _compress_weights(w1, w2, pe):
    z = jnp.zeros((2, CMP_STRIDE, HEAD_DIM, CMP_HIDDEN), w1.dtype)
    lo, hi = w1[:, :CMP_STRIDE], w1[:, CMP_STRIDE:]
    top = jnp.concatenate([lo, z, hi, z], axis=-1)
    bot = jnp.concatenate([z, lo, z, hi], axis=-1)
    wp = jnp.concatenate([top, bot], axis=2)
    wc = wp.reshape(2, CMP_STRIDE * KVD, 4 * CMP_HIDDEN).astype(BF16)
    z2 = jnp.zeros((2, CMP_HIDDEN, HEAD_DIM), w2.dtype)
    w2bd = jnp.concatenate([jnp.concatenate([w2, z2], axis=-1), jnp.concatenate([z2, w2], axis=-1)], axis=1)
    w1f = w1.reshape(2, CMP_LEN * HEAD_DIM, CMP_HIDDEN).astype(BF16)
    pef = pe.reshape(2, 1, CMP_LEN * HEAD_DIM).astype(BF16)
    return wc, w1f, pef, w2bd.astype(BF16)


def _compress_prompt_kernel(x_ref, wc_ref, w1f_ref, pef_ref, w2_ref, kc_ref, vct_ref, *, nch):
    def load_xp(t, p):
        return x_ref[:, p, t * KVD:(t + 1) * KVD]
    kc, vc = _compress(load_xp, wc_ref, w1f_ref, pef_ref, w2_ref, nch)
    kc_ref[...] = kc
    vct_ref[...] = vc.T.astype(BF16)


def _compress_prompt(kvf, n_batch, seq, cw):
    nch = seq // CMP_STRIDE
    wc, w1f, pef, w2bd = cw
    return pl.pallas_call(
        functools.partial(_compress_prompt_kernel, nch=nch),
        grid=(n_batch,),
        in_specs=[pl.BlockSpec((nch, CMP_STRIDE, 2 * KVD), lambda b: (b, 0, 0)),
                  _const_spec(wc.shape), _const_spec(w1f.shape), _const_spec(pef.shape), _const_spec(w2bd.shape)],
        out_specs=[pl.BlockSpec((nch, KVD), lambda b: (b, 0)), pl.BlockSpec((None, KVD, nch), lambda b: (b, 0, 0))],
        out_shape=[jax.ShapeDtypeStruct((n_batch * nch, KVD), F32), jax.ShapeDtypeStruct((n_batch, KVD, nch), BF16)],
        compiler_params=_cparams("parallel"), name="compress_prompt")(
            kvf.reshape(n_batch * nch, CMP_STRIDE, kvf.shape[1]), wc, w1f, pef, w2bd)


NEAR_CMP = 16


def _cmp_select_kernel(qt_ref, kc_ref, vct_ref, nt_ref, ovlt_ref, gl_ref, o_ref, selt_ref, s_scr, *, nch, n_blocks):
    i = pl.program_id(1)
    gates = jax.nn.sigmoid(gl_ref[0:N_GATE, :])
    cs = i * (Q_BLOCK // CMP_STRIDE) - NEAR_CMP // 2
    start = pl.multiple_of(jnp.maximum(cs, 0), 8)
    variant = jnp.minimum(i, 1)
    far_ok = lax.broadcasted_iota(jnp.int32, (nch, Q_BLOCK), 0) < cs
    kc_all = kc_ref[...].astype(BF16)
    kc_near = kc_ref[pl.ds(start, NEAR_CMP), :].astype(BF16)
    j_io = lax.broadcasted_iota(jnp.int32, (n_blocks, Q_BLOCK), 0)
    q_io = lax.broadcasted_iota(jnp.int32, (n_blocks, Q_BLOCK), 1)
    cur = (i * Q_BLOCK + q_io) // SEL_BLOCK
    forced = (j_io == 0) | (j_io == cur) | (j_io == cur - 1)
    outs, sels = [], []
    for g in range(N_KV):
        qt = _q_group_t(qt_ref, g)
        gsl = slice(g * HEAD_DIM, (g + 1) * HEAD_DIM)
        s_far = _dot(kc_all[:, gsl], qt)
        s_scr[...] = jnp.concatenate([jnp.where(far_ok, sl, NEG) for sl in _head_slabs(s_far)], axis=1)
        s_scr[pl.ds(start, NEAR_CMP), :] = _heads_add(
            _dot(kc_near[:, gsl], qt), [nt_ref[variant, g * GROUP + r] for r in range(GROUP)])
        s = s_scr[...]
        m = jnp.max(s, axis=0, keepdims=True)
        m = jnp.where(m > 0.5 * NEG, m, 0.0)
        e = jnp.exp2(s - m)
        den = jnp.sum(e, axis=0, keepdims=True)
        inv = 1.0 / jnp.where(den > 0, den, 1.0)
        outs += _head_slabs(_dot(vct_ref[gsl, :], e.astype(BF16)) * (inv * _gate_row(gates, 0, g)))
        p = e * inv
        pg = None
        for sl in _head_slabs(p):
            pg = sl if pg is None else pg + sl
        imp = _dot_01_by_f32(ovlt_ref[...], pg)
        score = jnp.where(j_io <= cur, jnp.where(forced, FORCED_SCORE, imp), NEG)
        sels.append(_topk_mask(score, min(N_SELECT, n_blocks), 0))
    o_ref[...] = jnp.concatenate(outs, axis=0).T
    selt_ref[...] = jnp.concatenate(sels, axis=0).astype(BF16)


def _overlap_np(nch, n_blocks, n_cols):
    ci = np.arange(nch)[:, None] * CMP_STRIDE
    sj = np.arange(n_cols)[None, :] * SEL_BLOCK
    ov = (ci < sj + SEL_BLOCK) & (ci + CMP_LEN > sj) & (np.arange(n_cols)[None, :] < n_blocks)
    ov[nch - 1] = False
    return ov.astype(np.float32)


def _cmp_select(qt, kc, vct, near_tiles, gate_t, n_batch, seq):
    nq = seq // Q_BLOCK
    nch = seq // CMP_STRIDE
    n_blocks = seq // SEL_BLOCK
    n = n_batch * seq
    ovlt = jnp.asarray(_overlap_np(nch, n_blocks, n_blocks).T, BF16)
    return pl.pallas_call(
        functools.partial(_cmp_select_kernel, nch=nch, n_blocks=n_blocks),
        grid=(n_batch, nq),
        in_specs=[_qblock_cols(HD, nq),
                  pl.BlockSpec((nch, KVD), lambda b, i: (b, 0)),
                  pl.BlockSpec((None, KVD, nch), lambda b, i: (b, 0, 0)),
                  _const_spec(near_tiles.shape), _const_spec(ovlt.shape), _qblock_cols(LANE, nq)],
        out_specs=[_qblock_rows(HD, nq), _qblock_cols(N_KV * n_blocks, nq)],
        out_shape=[jax.ShapeDtypeStruct((n, HD), F32), jax.ShapeDtypeStruct((N_KV * n_blocks, n), BF16)],
        scratch_shapes=[pltpu.VMEM((nch, GROUP * Q_BLOCK), F32)],
        compiler_params=_cparams("parallel", "parallel"), name="cmp_select")(
            qt, kc, vct, near_tiles, ovlt, gate_t)


FAR_CHUNK = 512
FAR_UNROLL = 1


def _selected_kernel(qt_ref, k_ref, vt_ref, selt_ref, ext_ref, bt_ref, gl_ref, o_ref, *, n_blocks):
    i = pl.program_id(1)
    gates = jax.nn.sigmoid(gl_ref[0:N_GATE, :])
    far_end = (i - 1) * Q_BLOCK
    per = FAR_CHUNK // Q_BLOCK
    n_far = (i + per - 2) // per
    outs = []
    for g in range(N_KV):
        qt = _q_group_t(qt_ref, g)
        gsl = slice(g * HEAD_DIM, (g + 1) * HEAD_DIM)
        sel_g = selt_ref[g * n_blocks:(g + 1) * n_blocks, :]

        def mask_bias(start, size):
            return (_dot(ext_ref[pl.ds(start, size), :], sel_g) - 1.0) * (-NEG)

        s_parts, v_parts = [], []
        for j in (1, 0):
            start = pl.multiple_of(jnp.maximum(i - j, 0) * Q_BLOCK, Q_BLOCK)
            mb = mask_bias(start, Q_BLOCK)
            if j > 0:
                mb = mb + jnp.where(i >= j, 0.0, NEG)
            sj = _dot(k_ref[pl.ds(start, Q_BLOCK), gsl], qt)
            s_parts.append(_heads_add(sj, [bt_ref[j, g * GROUP + r] + mb for r in range(GROUP)]))
            v_parts.append(vt_ref[gsl, pl.ds(start, Q_BLOCK)])
        s = jnp.concatenate(s_parts, axis=0)
        m0 = jnp.max(s, axis=0, keepdims=True)
        e = jnp.exp2(s - m0)
        l0 = jnp.sum(e, axis=0, keepdims=True)
        acc0 = _dot(jnp.concatenate(v_parts, axis=1), e.astype(BF16))

        def far_chunk(c, carry):
            m_old, l_old, acc_old = carry
            start = pl.multiple_of(jnp.minimum(c, n_far - 1) * FAR_CHUNK, FAR_CHUNK)
            kpos = c * FAR_CHUNK + lax.broadcasted_iota(jnp.int32, (FAR_CHUNK, Q_BLOCK), 0)
            mb = jnp.where(kpos < far_end, mask_bias(start, FAR_CHUNK), NEG)
            sc = _heads_add(_dot(k_ref[pl.ds(start, FAR_CHUNK), gsl], qt), [mb] * GROUP)
            m_new = jnp.maximum(m_old, jnp.max(sc, axis=0, keepdims=True))
            alpha = jnp.exp2(m_old - m_new)
            p = jnp.exp2(sc - m_new)
            l_new = alpha * l_old + jnp.sum(p, axis=0, keepdims=True)
            acc_new = alpha * acc_old + _dot(vt_ref[gsl, pl.ds(start, FAR_CHUNK)], p.astype(BF16))
            return m_new, l_new, acc_new

        def far_step(c2, carry):
            for u in range(FAR_UNROLL):
                carry = far_chunk(c2 * FAR_UNROLL + u, carry)
            return carry

        _, den, acc = lax.fori_loop(0, (n_far + FAR_UNROLL - 1) // FAR_UNROLL, far_step, (m0, l0, acc0))
        outs += _head_slabs(acc * (_gate_row(gates, 1, g) / den))
    o_ref[...] = jnp.concatenate(outs, axis=0).T


def _selected(qt, kvb, vt, v_idx, selt, near_tiles, gate_t, n_batch, seq):
    assert seq % FAR_CHUNK == 0
    nq = seq // Q_BLOCK
    n_blocks = seq // SEL_BLOCK
    ext = jnp.asarray((np.arange(seq)[:, None] // SEL_BLOCK == np.arange(n_blocks)[None, :]).astype(np.float32), BF16)
    return pl.pallas_call(
        functools.partial(_selected_kernel, n_blocks=n_blocks),
        grid=(n_batch, nq),
        in_specs=[_qblock_cols(HD, nq),
                  pl.BlockSpec((seq, KVD), lambda b, i: (b, 2)),
                  pl.BlockSpec((KVD, seq), lambda b, i: (v_idx, b)),
                  _qblock_cols(N_KV * n_blocks, nq), _const_spec(ext.shape), _const_spec(near_tiles.shape),
                  _qblock_cols(LANE, nq)],
        out_specs=_qblock_rows(HD, nq),
        out_shape=jax.ShapeDtypeStruct((n_batch * seq, HD), F32),
        compiler_params=_cparams("parallel", "parallel"), name="selected_attn")(
            qt, kvb, vt, selt, ext, near_tiles, gate_t)


def _block_diag_q(q):
    z = jnp.zeros_like(q)
    row = lax.broadcasted_iota(jnp.int32, (N_HEADS, 2 * HEAD_DIM), 0)
    return jnp.where(row < GROUP, jnp.concatenate([q, z], axis=1), jnp.concatenate([z, q], axis=1))


def _pick_group_half(o_full):
    row = lax.broadcasted_iota(jnp.int32, (N_HEADS, HEAD_DIM), 0)
    return jnp.where(row < GROUP, o_full[:, :HEAD_DIM], o_full[:, HEAD_DIM:])


def _window_decode_kernel(*refs, n_samp, use_sink, use_gate):
    it = iter(refs)
    q_ref, nkv_ref, st_ref, br_ref, b0_ref = next(it), next(it), next(it), next(it), next(it)
    sink_ref = next(it) if use_sink else None
    gl_ref = next(it) if use_gate else None
    o_ref = next(it)
    for s_i in range(n_samp):
        qbd = _block_diag_q(q_ref[s_i])
        kt = st_ref[s_i, 0, 0].astype(BF16)
        vt = st_ref[s_i, 0, 1].astype(BF16)
        new = nkv_ref[s_i]
        s_buf = _dot(qbd, kt) + br_ref[...]
        s_new = jnp.sum(qbd.astype(F32) * new[:, 0:KVD], axis=1, keepdims=True) + b0_ref[...]
        m = jnp.maximum(jnp.max(s_buf, axis=1, keepdims=True), s_new)
        if use_sink:
            m = jnp.maximum(m, sink_ref[...])
        e_buf = jnp.exp(s_buf - m)
        e_new = jnp.exp(s_new - m)
        den = jnp.sum(e_buf, axis=1, keepdims=True) + e_new
        if use_sink:
            den = den + jnp.exp(sink_ref[...] - m)
        o_full = _dot_nt(e_buf.astype(BF16), vt) + e_new * new[:, KVD:2 * KVD]
        f = 1.0 / den
        if use_gate:
            f = f * jax.nn.sigmoid(gl_ref[s_i])
        o_ref[s_i] = _pick_group_half(o_full) * f


def _window_decode(q, new_kv, state, layer, bias_row, bias0, sinks, gate_logits):
    db, lb = state.shape[0], state.shape[-1]
    n_samp = 8 if db % 8 == 0 else 1
    use_sink = sinks is not None
    use_gate = gate_logits is not None
    ins = [q, new_kv.reshape(db, 1, 2 * KVD), state, bias_row, bias0]
    in_specs = [pl.BlockSpec((n_samp, N_HEADS, HEAD_DIM), lambda t: (t, 0, 0)),
                pl.BlockSpec((n_samp, 1, 2 * KVD), lambda t: (t, 0, 0)),
                pl.BlockSpec((n_samp, 1, 2, KVD, lb), lambda t: (t, layer, 0, 0, 0)),
                _const_spec(bias_row.shape), _const_spec(bias0.shape)]
    if use_sink:
        ins.append(sinks.astype(F32).reshape(N_HEADS, 1))
        in_specs.append(_const_spec((N_HEADS, 1)))
    if use_gate:
        ins.append(gate_logits)
        in_specs.append(pl.BlockSpec((n_samp, N_HEADS, 1), lambda t: (t, 0, 0)))
    return pl.pallas_call(
        functools.partial(_window_decode_kernel, n_samp=n_samp, use_sink=use_sink, use_gate=use_gate),
        grid=(db // n_samp,), in_specs=in_specs,
        out_specs=pl.BlockSpec((n_samp, N_HEADS, HEAD_DIM), lambda t: (t, 0, 0)),
        out_shape=jax.ShapeDtypeStruct((db, N_HEADS, HEAD_DIM), F32),
        compiler_params=_cparams("parallel"), name="window_decode")(*ins)


N_KIND = 4
XT_ROWS = 1024


def _nsa_decode_kernel(pt_ref, q_ref, nkv_ref, gl_ref, pool_ref, wc_ref, w1f_ref, pef_ref, w2_ref,
                       cb_ref, sb_ref, b0_ref, ovl_ref, ex_ref, oc_ref, os_ref,
                       buf_a, buf_b, xrow, sem, *, layer, n_pages, past, n_blocks, nbp):
    step = pl.program_id(0)
    n_steps = pl.num_programs(0)
    nch = past // CMP_STRIDE
    n_past = past // SEL_BLOCK

    def page_copy(page, j, buf, slot):
        return pltpu.make_async_copy(
            pool_ref.at[page, layer], buf.at[:, :, pl.ds(j * PAGE_SIZE, PAGE_SIZE)], sem.at[slot])

    def start_fetch(sample, buf, slot):
        def body(j, c):
            page_copy(pt_ref[sample * n_pages + j], j, buf, slot).start()
            return c
        lax.fori_loop(0, n_pages, body, 0)

    def wait_fetch(buf, slot):
        def body(j, c):
            page_copy(0, j, buf, slot).wait()
            return c
        lax.fori_loop(0, n_pages, body, 0)

    def compute(buf, s_i):
        qbd = _block_diag_q(q_ref[s_i])
        new = nkv_ref[s_i]
        gates = jax.nn.sigmoid(gl_ref[s_i])

        def to_rows(t):
            xr = min(XT_ROWS, past)
            for c in range(past // xr):
                xrow[t, c * xr:(c + 1) * xr, :] = buf[t, :, c * xr:(c + 1) * xr].T

        def load_xp(t, p):
            return xrow[t, pl.ds(p, nch, stride=CMP_STRIDE), :]
        kc, vc = _compress(load_xp, wc_ref, w1f_ref, pef_ref, w2_ref, nch, prepare=to_rows)
        s = _dot_nt(qbd, kc.astype(BF16)) + cb_ref[...]
        m = jnp.max(s, axis=1, keepdims=True)
        m = jnp.where(m > 0.5 * NEG, m, 0.0)
        e = jnp.exp(s - m)
        den = jnp.sum(e, axis=1, keepdims=True)
        inv = 1.0 / jnp.where(den > 0, den, 1.0)
        oc_full = _dot(e.astype(BF16), vc.astype(BF16))
        oc_ref[s_i] = _pick_group_half(oc_full) * (inv * gates[0])
        p = e * inv
        j_io = lax.broadcasted_iota(jnp.int32, (1, nbp), 1)
        cur = past // SEL_BLOCK
        forced = (j_io == 0) | (j_io == cur) | (j_io == cur - 1)
        sel_rows = []
        for g in range(N_KV):
            pg = jnp.sum(p[g * GROUP:(g + 1) * GROUP], axis=0, keepdims=True)
            imp = _dot_f32_by_01(pg, ovl_ref[...])
            score = jnp.where(j_io <= cur, jnp.where(forced, FORCED_SCORE, imp), NEG)
            sel_rows.append(_topk_mask_row(score, min(N_SELECT, n_blocks))[:, 0:n_past])
        row = lax.broadcasted_iota(jnp.int32, (N_HEADS, n_past), 0)
        sel16 = jnp.where(row < GROUP, sel_rows[0], sel_rows[1]).astype(BF16)
        selm = _dot(sel16, ex_ref[...])
        s2 = _dot(qbd, buf[2].astype(BF16)) + sb_ref[...] + (selm - 1.0) * (-NEG)
        s_new = jnp.sum(qbd.astype(F32) * new[:, 0:KVD], axis=1, keepdims=True) + b0_ref[...]
        m2 = jnp.maximum(jnp.max(s2, axis=1, keepdims=True), s_new)
        e2 = jnp.exp(s2 - m2)
        e_new = jnp.exp(s_new - m2)
        den2 = jnp.sum(e2, axis=1, keepdims=True) + e_new
        os_full = _dot_nt(e2.astype(BF16), buf[3].astype(BF16)) + e_new * new[:, KVD:2 * KVD]
        os_ref[s_i] = _pick_group_half(os_full) * (gates[1] / den2)

    @pl.when(step == 0)
    def _():
        start_fetch(0, buf_a, 0)

    start_fetch(2 * step + 1, buf_b, 1)
    wait_fetch(buf_a, 0)
    compute(buf_a, 0)

    @pl.when(step + 1 < n_steps)
    def _():
        start_fetch(2 * step + 2, buf_a, 0)

    wait_fetch(buf_b, 1)
    compute(buf_b, 1)


def _nsa_decode(q, new_sel, gate_logits2, pool, page_table, layer, cw, cmp_bias, sel_bias, bias0):
    db, n_pages = page_table.shape
    assert db % 2 == 0
    past = n_pages * PAGE_SIZE
    nch = past // CMP_STRIDE
    n_past = past // SEL_BLOCK
    n_blocks = n_past + 1
    nbp = -(-n_blocks // LANE) * LANE
    wc, w1f, pef, w2bd = cw
    ovl = jnp.asarray(_overlap_np(nch, n_blocks, nbp), BF16)
    ex = jnp.asarray((np.arange(n_past)[:, None] == (np.arange(past)[None, :] // SEL_BLOCK)).astype(np.float32), BF16)
    consts = [wc, w1f, pef, w2bd, cmp_bias, sel_bias, bias0, ovl, ex]
    per2 = lambda *tail: pl.BlockSpec((2,) + tail, lambda t, pt: (t,) + (0,) * len(tail))
    grid_spec = pltpu.PrefetchScalarGridSpec(
        num_scalar_prefetch=1, grid=(db // 2,),
        in_specs=[per2(N_HEADS, HEAD_DIM), per2(1, 2 * KVD), per2(2, N_HEADS, 1),
                  pl.BlockSpec(memory_space=pl.ANY)] + [_const_spec(c.shape) for c in consts],
        out_specs=[per2(N_HEADS, HEAD_DIM), per2(N_HEADS, HEAD_DIM)],
        scratch_shapes=[pltpu.VMEM((N_KIND, KVD, past), F32), pltpu.VMEM((N_KIND, KVD, past), F32),
                        pltpu.VMEM((2, past, KVD), F32), pltpu.SemaphoreType.DMA((2,))])
    out = jax.ShapeDtypeStruct((db, N_HEADS, HEAD_DIM), F32)
    return pl.pallas_call(
        functools.partial(_nsa_decode_kernel, layer=layer, n_pages=n_pages, past=past,
                          n_blocks=n_blocks, nbp=nbp),
        grid_spec=grid_spec, out_shape=[out, out],
        compiler_params=_cparams("arbitrary"), name="nsa_decode")(
            page_table.reshape(-1), q, new_sel.reshape(db, 1, 2 * KVD), gate_logits2, pool, *consts)


def _shift_kernel(st_ref, new_ref, o_ref, *, lb):
    x = st_ref[...]
    lane = lax.broadcasted_iota(jnp.int32, x.shape, 2)
    o_ref[...] = jnp.where(lane == lb - 1, new_ref[...], pltpu.roll(x, lb - 1, 2))


def _shift_state(state, new_rows):
    db, w, lb = state.shape
    ns = max(1, min(db, (4 << 20) // (w * lb * 4)))
    assert db % ns == 0
    blk = pl.BlockSpec((ns, w, lb), lambda t: (t, 0, 0))
    return pl.pallas_call(
        functools.partial(_shift_kernel, lb=lb), grid=(db // ns,),
        in_specs=[blk, pl.BlockSpec((ns, w, 1), lambda t: (t, 0, 0))], out_specs=blk,
        out_shape=jax.ShapeDtypeStruct((db, w, lb), state.dtype),
        compiler_params=_cparams("parallel"), name="shift_state")(state, new_rows.reshape(db, w, 1))


def _decode_bias_row(rel_bias, past, kpos, window):
    dist = past - kpos
    valid = (dist >= 0) if window is None else ((dist >= 0) & (dist < window))
    return _bias_tile(rel_bias, dist[None, :], valid[None, :], False)[:, 0, :]


def kernel(x_prompt, x_sample, state_swa, cache_nsa, state_nsa_win, page_table, rel_bias, norm_mix, norm_mlp,
           norm_final, w_in_a, b_in_a, w_out_a, b_out_a, sinks_a, w_in_b, w_out_b, w_cmp1, w_cmp2, pe_cmp,
           w_up, w_down):
    n_batch, seq, _ = x_prompt.shape
    db = x_sample.shape[0]
    assert x_sample.shape[1] == 1
    depth = norm_mix.shape[0]
    n_la, n_lb = w_in_a.shape[0], w_in_b.shape[0]
    lba, lbb = state_swa.shape[1], state_nsa_win.shape[1]
    n_pages = page_table.shape[1]
    past = n_pages * PAGE_SIZE
    assert seq >= WIN_B and lba == WIN_A and lbb == WIN_B and past >= WIN_B

    kk = np.arange(Q_BLOCK)[:, None]
    qq = np.arange(Q_BLOCK)[None, :]
    d0, d1 = qq - kk, Q_BLOCK + qq - kk
    swa_tiles = LOG2E * jnp.stack([_bias_tile(rel_bias, d0, (d0 >= 0) & (d0 < WIN_A), False),
                                   _bias_tile(rel_bias, d1, (d1 >= 0) & (d1 < WIN_A), False)])
    near_tiles = LOG2E * jnp.stack([_bias_tile(rel_bias, d0, d0 >= 0, True),
                                    _bias_tile(rel_bias, d1, d1 >= 0, True)])
    nback_b = WIN_B // Q_BLOCK
    far_mask = jnp.asarray(np.where(kk > qq, 0.0, NEG), F32)
    no_mask = jnp.zeros((8, LANE), F32)
    cc = np.arange(NEAR_CMP)[:, None]
    dc0 = qq - CMP_STRIDE * cc - (CMP_LEN - 1)
    dc1 = qq - CMP_STRIDE * (cc - NEAR_CMP // 2) - (CMP_LEN - 1)
    cmp_tiles = LOG2E * jnp.stack([_bias_tile(rel_bias, dc0, dc0 >= 0, True),
                                   _bias_tile(rel_bias, dc1, dc1 >= 0, True)])
    nch_d = past // CMP_STRIDE
    cmp_row = _decode_bias_row(rel_bias, past, np.arange(nch_d) * CMP_STRIDE + CMP_LEN - 1, None)
    sel_row = _decode_bias_row(rel_bias, past, np.arange(past), None)
    swa_row = _decode_bias_row(rel_bias, past, past - lba + np.arange(lba), WIN_A)
    win_row = _decode_bias_row(rel_bias, past, past - lbb + np.arange(lbb), WIN_B)
    bias0 = rel_bias.astype(F32)[0].reshape(N_HEADS, 1)

    xp = x_prompt.reshape(n_batch * seq, D_MODEL)
    xs = x_sample.reshape(db, D_MODEL)
    to_minor = lambda a: jnp.transpose(a, (0, 2, 3, 4, 5, 1))
    st_swa = to_minor(state_swa).reshape(db, n_la, 2, KVD, lba)
    st_win = to_minor(state_nsa_win).reshape(db, n_lb, 2, KVD, lbb)
    pool = to_minor(cache_nsa).reshape(cache_nsa.shape[0], n_lb, N_KIND, KVD, PAGE_SIZE)
    swa_p, swa_new, rows_s, win_p, win_new = [], [], [], [], []
    rows_pt = jnp.zeros((n_batch, n_lb, N_KIND * KVD, seq), F32)

    for layer in range(depth):
        final = norm_final if layer == depth - 1 else None
        if layer % 2 == 0:
            a = layer // 2
            qt_p, kv_p, kvb_p, vt_p = _project(xp, norm_mix[layer], w_in_a[a], b_in_a[a], 2 * KVD, False, (KVD,))
            q_s, kv_s, _ = _project(xs, norm_mix[layer], w_in_a[a], b_in_a[a], 2 * KVD, False, None)
            o_p = _banded(qt_p, kvb_p, 0, vt_p, 0, n_batch, seq, ("bias", "bias"), swa_tiles, no_mask,
                          sinks_a[a], None, None)
            o_s = _window_decode(jnp.swapaxes(q_s, 0, 1), kv_s, st_swa, a, swa_row, bias0, sinks_a[a], None)
            swa_p.append(kv_p.reshape(n_batch, seq, 2, N_KV, HEAD_DIM)[:, seq - WIN_A:])
            swa_new.append(kv_s)
            xp = _out_mlp(xp, [o_p], w_out_a[a], b_out_a[a], norm_mlp[layer], w_up[layer], w_down[layer], final)
            xs = _out_mlp(xs, [o_s.reshape(db, HD)], w_out_a[a], b_out_a[a], norm_mlp[layer], w_up[layer],
                          w_down[layer], final)
        else:
            b = layer // 2
            cw = _compress_weights(w_cmp1[b], w_cmp2[b], pe_cmp[b])
            w_in = jnp.pad(w_in_b[b], ((0, 0), (0, LANE - N_GATE)))
            qt_p, kv_p, kvb_p, vt_p, glt_p, rows_pt = _project(xp, norm_mix[layer], w_in, None, 6 * KVD, True,
                                                               (3 * KVD, 5 * KVD), (rows_pt, b))
            q_s, kv_s, _, gl_s = _project(xs, norm_mix[layer], w_in, None, 6 * KVD, True, None)
            kc, vct = _compress_prompt(kv_p, n_batch, seq, cw)
            o_c, selt = _cmp_select(qt_p, kc, vct, cmp_tiles, glt_p, n_batch, seq)
            o_sel = _selected(qt_p, kvb_p, vt_p, 0, selt, near_tiles, glt_p, n_batch, seq)
            kinds = ("bias", "bias") + ("none",) * (nback_b - 2) + ("mask",)
            o_w = _banded(qt_p, kvb_p, 4, vt_p, 1, n_batch, seq, kinds, near_tiles, far_mask, None, glt_p, 2)
            win_p.append(kv_p.reshape(n_batch, seq, 6 * KVD)[:, seq - WIN_B:, 4 * KVD:]
                         .reshape(n_batch, WIN_B, 2, N_KV, HEAD_DIM))
            xp = _out_mlp(xp, [o_c, o_sel, o_w], w_out_b[b], None, norm_mlp[layer], w_up[layer], w_down[layer], final)
            q_sd = jnp.swapaxes(q_s, 0, 1)
            gl_s3 = gl_s.reshape(db, 3, N_HEADS, 1)
            oc_s, os_s = _nsa_decode(q_sd, kv_s[:, 2 * KVD:4 * KVD], gl_s3[:, 0:2], pool, page_table, b, cw,
                                     cmp_row, sel_row, bias0)
            ow_s = _window_decode(q_sd, kv_s[:, 4 * KVD:], st_win, b, win_row, bias0, None, gl_s3[:, 2])
            rows_s.append(kv_s[:, :4 * KVD].reshape(db, 1, 4, N_KV, HEAD_DIM))
            win_new.append(kv_s[:, 4 * KVD:])
            xs = _out_mlp(xs, [oc_s.reshape(db, HD), os_s.reshape(db, HD), ow_s.reshape(db, HD)], w_out_b[b], None,
                          norm_mlp[layer], w_up[layer], w_down[layer], final)

    y_prompt = xp.reshape(n_batch, seq, D_MODEL)
    y_sample = xs.reshape(db, 1, D_MODEL)
    from_minor = lambda a, nl, lb: jnp.transpose(a.reshape(db, nl, 2, N_KV, HEAD_DIM, lb), (0, 5, 1, 2, 3, 4))
    swa_sample = _shift_state(st_swa.reshape(db, n_la * 2 * KVD, lba), jnp.concatenate(swa_new, axis=1))
    win_sample = _shift_state(st_win.reshape(db, n_lb * 2 * KVD, lbb), jnp.concatenate(win_new, axis=1))
    return (y_prompt, y_sample,
            jnp.stack(swa_p, axis=2),
            from_minor(swa_sample, n_la, lba),
            jnp.transpose(rows_pt.reshape(n_batch, n_lb, N_KIND, N_KV, HEAD_DIM, seq), (0, 5, 1, 2, 3, 4)),
            jnp.stack(rows_s, axis=2),
            jnp.stack(win_p, axis=2),
            from_minor(win_sample, n_lb, lbb))
```

```python
import functools
import math

import numpy as np
import jax
import jax.numpy as jnp
from jax import lax
from jax.experimental import pallas as pl
from jax.experimental.pallas import tpu as pltpu

D_MODEL = 1024
N_HEADS = 16
HEAD_DIM = 64
N_KV = 2
GROUP = N_HEADS // N_KV
HD = N_HEADS * HEAD_DIM
KVD = N_KV * HEAD_DIM
D_FF = 4 * D_MODEL
PAGE_SIZE = 128
WIN_A = 128
WIN_B = 512
Q_BLOCK = 128
CMP_STRIDE = 16
CMP_LEN = 2 * CMP_STRIDE
CMP_HIDDEN = 128
SEL_BLOCK = 64
N_SELECT = 16
FORCED_SCORE = 1e9
N_BUCKETS = 32
MAX_DISTANCE = 128
NORM_EPS = 1e-5
SCALE = HEAD_DIM ** -0.5
LOG2E = math.log2(math.e)
N_GATE = 3 * N_HEADS
LANE = 128

NEG = -1e30
REMOVED = -2e30
F32 = jnp.float32
BF16 = jnp.bfloat16
VMEM_LIMIT = 56 * 1024 * 1024


def _cparams(*sem):
    return pltpu.CompilerParams(dimension_semantics=sem, vmem_limit_bytes=VMEM_LIMIT)


def _const_spec(shape):
    nd = len(shape)
    return pl.BlockSpec(shape, lambda *_: (0,) * nd, pipeline_mode=pl.Buffered(1))


def _bucket_np(dist):
    n = np.maximum(dist, 0)
    exact = N_BUCKETS // 2
    nf = np.maximum(n, exact).astype(np.float32)
    large = exact + (np.log(nf / np.float32(exact)) / np.float32(math.log(MAX_DISTANCE / exact))
                     * np.float32(N_BUCKETS - exact)).astype(np.int32)
    return np.where(n < exact, n, np.minimum(large, N_BUCKETS - 1)).astype(np.int32)


def _bias_tile(rel_bias, dist, valid, shift):
    tb = rel_bias.astype(F32)
    if shift:
        tb = tb - tb[N_BUCKETS - 1][None, :]
    bucket = jnp.asarray(_bucket_np(dist).reshape(1, -1))
    onehot = (bucket == jnp.arange(N_BUCKETS, dtype=jnp.int32)[:, None]).astype(F32)
    b = jnp.dot(tb.T, onehot, precision=lax.Precision.HIGHEST).reshape((N_HEADS,) + dist.shape)
    return jnp.where(jnp.asarray(valid)[None], b, NEG)


def _rms(x, g):
    ms = jnp.mean(x * x, axis=-1, keepdims=True)
    return x * lax.rsqrt(ms + NORM_EPS) * g


def _dot(a, b):
    return jnp.dot(a, b, preferred_element_type=F32)


def _dot_nt(a, b):
    return lax.dot_general(a, b, (((1,), (1,)), ((), ())), preferred_element_type=F32)


def _split3(a):
    hi = a.astype(BF16)
    r1 = a - hi.astype(F32)
    mid = r1.astype(BF16)
    lo = (r1 - mid.astype(F32)).astype(BF16)
    return hi, mid, lo


def _dot_f32_by_01(a, b01):
    hi, mid, lo = _split3(a)
    return _dot(hi, b01) + _dot(mid, b01) + _dot(lo, b01)


def _dot_01_by_f32(b01, a):
    hi, mid, lo = _split3(a)
    return _dot(b01, hi) + _dot(b01, mid) + _dot(b01, lo)


def _topk_mask(score, k, axis):
    n = score.shape[axis]
    iota = lax.broadcasted_iota(jnp.int32, score.shape, axis).astype(F32)
    sel = jnp.zeros_like(score)
    s = score
    for _ in range(k):
        m = jnp.max(s, axis=axis, keepdims=True)
        first = jnp.min(jnp.where(s == m, iota, float(n)), axis=axis, keepdims=True)
        hit = iota == first
        ok = jnp.where(m > 0.5 * NEG, 1.0, 0.0)
        sel = jnp.where(hit, ok, sel)
        s = jnp.where(hit, REMOVED, s)
    return sel


def _topk_mask_row(score, k):
    n = score.shape[1]
    s_j = jnp.broadcast_to(score, (n, n))
    s_i = jnp.concatenate([jnp.broadcast_to(score, (LANE, n)).T] * (n // LANE), axis=1)
    i_io = lax.broadcasted_iota(jnp.int32, (n, n), 0)
    j_io = lax.broadcasted_iota(jnp.int32, (n, n), 1)
    ahead = jnp.where(s_i > s_j, 1.0, jnp.where(s_i == s_j, jnp.where(i_io < j_io, 1.0, 0.0), 0.0))
    rank = jnp.sum(ahead, axis=0, keepdims=True)
    return jnp.where(rank < k, jnp.where(score > 0.5 * NEG, 1.0, 0.0), 0.0)


def _q_group_t(qt_ref, g):
    return jnp.concatenate(
        [qt_ref[(g * GROUP + r) * HEAD_DIM:(g * GROUP + r + 1) * HEAD_DIM, :] for r in range(GROUP)], axis=1)


def _heads_add(s, tiles):
    return jnp.concatenate([s[:, r * Q_BLOCK:(r + 1) * Q_BLOCK] + tiles[r] for r in range(GROUP)], axis=1)


def _head_row(rows):
    return jnp.concatenate(rows, axis=1)


def _head_slabs(acc_t):
    return [acc_t[:, r * Q_BLOCK:(r + 1) * Q_BLOCK] for r in range(GROUP)]


def _proj_kernel(*refs, n_kv, has_bias, has_gate, v_cols, n_rows):
    it = iter(refs)
    x_ref, g_ref, w_ref = next(it), next(it), next(it)
    b_ref = next(it) if has_bias else None
    if n_rows:
        next(it)
    q_ref, kv_ref, kvb_ref = next(it), next(it), next(it)
    vt_ref = next(it) if v_cols is not None else None
    gl_ref = next(it) if has_gate else None
    rows_ref = next(it) if n_rows else None
    xn = _rms(x_ref[...], g_ref[...]).astype(BF16)
    u = _dot(xn, w_ref[...])
    if has_bias:
        u = u + b_ref[...]
    kv = u[:, HD:HD + n_kv]
    kv_ref[...] = kv
    kvb_ref[...] = kv.astype(BF16)
    if v_cols is None:
        for h in range(N_HEADS):
            q_ref[h] = (u[:, h * HEAD_DIM:(h + 1) * HEAD_DIM] * SCALE).astype(BF16)
        if has_gate:
            gl_ref[...] = u[:, HD + n_kv:HD + n_kv + N_GATE]
    else:
        q_ref[...] = (u[:, :HD] * (SCALE * LOG2E)).T.astype(BF16)
        for idx, c0 in enumerate(v_cols):
            vt_ref[idx * KVD:(idx + 1) * KVD, :] = u[:, HD + c0:HD + c0 + KVD].T.astype(BF16)
        if has_gate:
            gl_ref[...] = u[:, HD + n_kv:HD + n_kv + LANE].T
        if n_rows:
            rows_ref[...] = u[:, HD:HD + n_rows].T


def _project(x, g, w, b, n_kv, has_gate, v_cols, rows_into=None):
    n = x.shape[0]
    dout = w.shape[1]
    tm = min(512, n)
    assert n % tm == 0
    has_bias = b is not None
    prompt = v_cols is not None
    ins = [x, g.reshape(1, D_MODEL), w.astype(BF16)]
    in_specs = [pl.BlockSpec((tm, D_MODEL), lambda t: (t, 0)),
                _const_spec((1, D_MODEL)), _const_spec((D_MODEL, dout))]
    if has_bias:
        ins.append(b.reshape(1, dout))
        in_specs.append(_const_spec((1, dout)))
    aliases = {}
    n_rows = 0
    if rows_into is not None:
        rows_arr, rows_layer = rows_into
        n_rows, seq = rows_arr.shape[2], rows_arr.shape[3]
        assert seq % tm == 0
        aliases = {len(ins): 0}
        ins.append(rows_arr)
        in_specs.append(pl.BlockSpec(memory_space=pl.ANY))
    rows = lambda wd, dt: (jax.ShapeDtypeStruct((n, wd), dt), pl.BlockSpec((tm, wd), lambda t: (t, 0)))
    cols = lambda ht, dt: (jax.ShapeDtypeStruct((ht, n), dt), pl.BlockSpec((ht, tm), lambda t: (0, t)))
    if prompt:
        outs = [cols(HD, BF16), rows(n_kv, F32), rows(n_kv, BF16), cols(len(v_cols) * KVD, BF16)]
        if has_gate:
            outs.append(cols(LANE, F32))
    else:
        outs = [(jax.ShapeDtypeStruct((N_HEADS, n, HEAD_DIM), BF16),
                 pl.BlockSpec((N_HEADS, tm, HEAD_DIM), lambda t: (0, t, 0))),
                rows(n_kv, F32), rows(n_kv, BF16)]
        if has_gate:
            outs.append(rows(N_GATE, F32))
    if n_rows:
        spb = seq // tm
        aliases = {k: len(outs) for k in aliases}
        outs.append((jax.ShapeDtypeStruct(rows_arr.shape, rows_arr.dtype),
                     pl.BlockSpec((None, None, n_rows, tm), lambda t: (t // spb, rows_layer, 0, t % spb))))
    return pl.pallas_call(
        functools.partial(_proj_kernel, n_kv=n_kv, has_bias=has_bias, has_gate=has_gate, v_cols=v_cols,
                          n_rows=n_rows),
        grid=(n // tm,), in_specs=in_specs, out_specs=[o[1] for o in outs], out_shape=[o[0] for o in outs],
        input_output_aliases=aliases,
        compiler_params=_cparams("parallel"), name="norm_proj")(*ins)


FF_CHUNK = 1024


def _out_mlp_kernel(*refs, n_o, has_bias, final):
    it = iter(refs)
    x_ref = next(it)
    o_refs = [next(it) for _ in range(n_o)]
    wo_ref = next(it)
    bo_ref = next(it) if has_bias else None
    gm_ref, wup_ref, wdn_ref = next(it), next(it), next(it)
    gf_ref = next(it) if final else None
    y_ref = next(it)
    o = o_refs[0][...]
    for r in o_refs[1:]:
        o = o + r[...]
    x1 = x_ref[...] + _dot(o.astype(BF16), wo_ref[...])
    if has_bias:
        x1 = x1 + bo_ref[...]
    xn = _rms(x1, gm_ref[...]).astype(BF16)
    acc = x1
    for c in range(D_FF // FF_CHUNK):
        h = jnp.maximum(_dot(xn, wup_ref[:, c * FF_CHUNK:(c + 1) * FF_CHUNK]), 0.0)
        acc = acc + _dot((h * h).astype(BF16), wdn_ref[c * FF_CHUNK:(c + 1) * FF_CHUNK, :])
    y_ref[...] = _rms(acc, gf_ref[...]) if final else acc


def _out_mlp(x, os_, w_out, b_out, g_mlp, w_up, w_down, g_final):
    n = x.shape[0]
    tm = min(512, n)
    assert n % tm == 0
    has_bias = b_out is not None
    final = g_final is not None
    row = pl.BlockSpec((tm, D_MODEL), lambda t: (t, 0))
    ins = [x, *os_, w_out.astype(BF16)]
    in_specs = [row] + [row] * len(os_) + [_const_spec((HD, D_MODEL))]
    if has_bias:
        ins.append(b_out.reshape(1, D_MODEL))
        in_specs.append(_const_spec((1, D_MODEL)))
    ins += [g_mlp.reshape(1, D_MODEL), w_up.astype(BF16), w_down.astype(BF16)]
    in_specs += [_const_spec((1, D_MODEL)), _const_spec((D_MODEL, D_FF)), _const_spec((D_FF, D_MODEL))]
    if final:
        ins.append(g_final.reshape(1, D_MODEL))
        in_specs.append(_const_spec((1, D_MODEL)))
    return pl.pallas_call(
        functools.partial(_out_mlp_kernel, n_o=len(os_), has_bias=has_bias, final=final),
        grid=(n // tm,), in_specs=in_specs, out_specs=row,
        out_shape=jax.ShapeDtypeStruct((n, D_MODEL), F32),
        compiler_params=_cparams("parallel"), name="out_mlp")(*ins)


def _qblock_cols(height, nq):
    return pl.BlockSpec((height, Q_BLOCK), lambda b, i: (0, b * nq + i))


def _qblock_rows(width, nq):
    return pl.BlockSpec((Q_BLOCK, width), lambda b, i: (b * nq + i, 0))


def _gate_row(gates, branch, g):
    return _head_row([gates[branch * N_HEADS + g * GROUP + r:branch * N_HEADS + g * GROUP + r + 1, :]
                      for r in range(GROUP)])


def _banded_kernel(*refs, kinds, use_sink, gate_branch):
    it = iter(refs)
    qt_ref, k_ref, vt_ref, bt_ref, mt_ref = next(it), next(it), next(it), next(it), next(it)
    sink_ref = next(it) if use_sink else None
    gl_ref = next(it) if gate_branch is not None else None
    o_ref = next(it)
    i = pl.program_id(1)
    gates = jax.nn.sigmoid(gl_ref[0:N_GATE, :]) if gate_branch is not None else None
    bias_slot = {}
    for j, kind in enumerate(kinds):
        if kind == "bias":
            bias_slot[j] = len(bias_slot)
    outs = []
    for g in range(N_KV):
        qt = _q_group_t(qt_ref, g)
        gsl = slice(g * HEAD_DIM, (g + 1) * HEAD_DIM)
        s_parts, v_parts = [], []
        for j, kind in enumerate(kinds):
            start = pl.multiple_of(jnp.maximum(i - j, 0) * Q_BLOCK, Q_BLOCK)
            v_parts.append(vt_ref[gsl, pl.ds(start, Q_BLOCK)])
            sj = _dot(k_ref[pl.ds(start, Q_BLOCK), gsl], qt)
            pen = jnp.where(i >= j, 0.0, NEG) if j > 0 else 0.0
            if kind == "bias":
                sj = _heads_add(sj, [bt_ref[bias_slot[j], g * GROUP + r] + pen for r in range(GROUP)])
            elif kind == "mask":
                sj = _heads_add(sj, [mt_ref[...] + pen] * GROUP)
            else:
                sj = sj + pen
            s_parts.append(sj)
        s = jnp.concatenate(s_parts, axis=0)
        vt = jnp.concatenate(v_parts, axis=1)
        m = jnp.max(s, axis=0, keepdims=True)
        if use_sink:
            sk = _head_row([jnp.full((1, Q_BLOCK), sink_ref[g * GROUP + r] * LOG2E, F32) for r in range(GROUP)])
            m = jnp.maximum(m, sk)
        e = jnp.exp2(s - m)
        den = jnp.sum(e, axis=0, keepdims=True)
        if use_sink:
            den = den + jnp.exp2(sk - m)
        f = 1.0 / den
        if gates is not None:
            f = f * _gate_row(gates, gate_branch, g)
        outs += _head_slabs(_dot(vt, e.astype(BF16)) * f)
    o_ref[...] = jnp.concatenate(outs, axis=0).T


def _banded(qt, kvb, k_col, vt, v_idx, n_batch, seq, kinds, bias_tiles, mask_tile, sinks, gate_t, gate_branch):
    nq = seq // Q_BLOCK
    use_sink = sinks is not None
    ins = [qt, kvb, vt, bias_tiles, mask_tile]
    in_specs = [_qblock_cols(HD, nq),
                pl.BlockSpec((seq, KVD), lambda b, i: (b, k_col)),
                pl.BlockSpec((KVD, seq), lambda b, i: (v_idx, b)),
                _const_spec(bias_tiles.shape), _const_spec(mask_tile.shape)]
    if use_sink:
        ins.append(sinks.astype(F32))
        in_specs.append(pl.BlockSpec(memory_space=pltpu.SMEM))
    if gate_t is not None:
        ins.append(gate_t)
        in_specs.append(_qblock_cols(LANE, nq))
    else:
        gate_branch = None
    return pl.pallas_call(
        functools.partial(_banded_kernel, kinds=kinds, use_sink=use_sink, gate_branch=gate_branch),
        grid=(n_batch, nq), in_specs=in_specs, out_specs=_qblock_rows(HD, nq),
        out_shape=jax.ShapeDtypeStruct((n_batch * seq, HD), F32),
        compiler_params=_cparams("parallel", "parallel"), name="banded_attn")(*ins)


def _compress(load_xp, wc_ref, w1f_ref, pef_ref, w2_ref, nch, prepare=None):
    outs = []
    if prepare is not None:
        for t in range(2):
            prepare(t)
    for t in range(2):
        xs = jnp.concatenate([load_xp(t, p).astype(BF16) for p in range(CMP_STRIDE)], axis=1)
        acc = _dot(xs, wc_ref[t])
        pet = _dot(pef_ref[t], w1f_ref[t])
        lo = acc[:, :2 * CMP_HIDDEN]
        hi_next = pltpu.roll(acc[:, 2 * CMP_HIDDEN:], nch - 1, 0)
        hh = lo + hi_next + jnp.concatenate([pet, pet], axis=1)
        a = (hh * jax.nn.sigmoid(hh)).astype(BF16)
        outs.append(_dot(a, w2_ref[t]))
    return outs


def _compress_weights(w1, w2, pe):
    z = jnp.zeros((2, CMP_STRIDE, HEAD_DIM, CMP_HIDDEN), w1.dtype)
    lo, hi = w1[:, :CMP_STRIDE], w1[:, CMP_STRIDE:]
    top = jnp.concatenate([lo, z, hi, z], axis=-1)
    bot = jnp.concatenate([z, lo, z, hi], axis=-1)
    wp = jnp.concatenate([top, bot], axis=2)
    wc = wp.reshape(2, CMP_STRIDE * KVD, 4 * CMP_HIDDEN).astype(BF16)
    z2 = jnp.zeros((2, CMP_HIDDEN, HEAD_DIM), w2.dtype)
    w2bd = jnp.concatenate([jnp.concatenate([w2, z2], axis=-1), jnp.concatenate([z2, w2], axis=-1)], axis=1)
    w1f = w1.reshape(2, CMP_LEN * HEAD_DIM, CMP_HIDDEN).astype(BF16)
    pef = pe.reshape(2, 1, CMP_LEN * HEAD_DIM).astype(BF16)
    return wc, w1f, pef, w2bd.astype(BF16)


def _compress_prompt_kernel(x_ref, wc_ref, w1f_ref, pef_ref, w2_ref, kc_ref, vct_ref, *, nch):
    def load_xp(t, p):
        return x_ref[:, p, t * KVD:(t + 1) * KVD]
    kc, vc = _compress(load_xp, wc_ref, w1f_ref, pef_ref, w2_ref, nch)
    kc_ref[...] = kc
    vct_ref[...] = vc.T.astype(BF16)


def _compress_prompt(kvf, n_batch, seq, cw):
    nch = seq // CMP_STRIDE
    wc, w1f, pef, w2bd = cw
    return pl.pallas_call(
        functools.partial(_compress_prompt_kernel, nch=nch),
        grid=(n_batch,),
        in_specs=[pl.BlockSpec((nch, CMP_STRIDE, 2 * KVD), lambda b: (b, 0, 0)),
                  _const_spec(wc.shape), _const_spec(w1f.shape), _const_spec(pef.shape), _const_spec(w2bd.shape)],
        out_specs=[pl.BlockSpec((nch, KVD), lambda b: (b, 0)), pl.BlockSpec((None, KVD, nch), lambda b: (b, 0, 0))],
        out_shape=[jax.ShapeDtypeStruct((n_batch * nch, KVD), F32), jax.ShapeDtypeStruct((n_batch, KVD, nch), BF16)],
        compiler_params=_cparams("parallel"), name="compress_prompt")(
            kvf.reshape(n_batch * nch, CMP_STRIDE, kvf.shape[1]), wc, w1f, pef, w2bd)


NEAR_CMP = 16


def _cmp_select_kernel(qt_ref, kc_ref, vct_ref, nt_ref, ovlt_ref, gl_ref, o_ref, selt_ref, s_scr, *, nch, n_blocks):
    i = pl.program_id(1)
    gates = jax.nn.sigmoid(gl_ref[0:N_GATE, :])
    cs = i * (Q_BLOCK // CMP_STRIDE) - NEAR_CMP // 2
    start = pl.multiple_of(jnp.maximum(cs, 0), 8)
    variant = jnp.minimum(i, 1)
    far_ok = lax.broadcasted_iota(jnp.int32, (nch, Q_BLOCK), 0) < cs
    kc_all = kc_ref[...].astype(BF16)
    kc_near = kc_ref[pl.ds(start, NEAR_CMP), :].astype(BF16)
    j_io = lax.broadcasted_iota(jnp.int32, (n_blocks, Q_BLOCK), 0)
    q_io = lax.broadcasted_iota(jnp.int32, (n_blocks, Q_BLOCK), 1)
    cur = (i * Q_BLOCK + q_io) // SEL_BLOCK
    forced = (j_io == 0) | (j_io == cur) | (j_io == cur - 1)
    outs, sels = [], []
    for g in range(N_KV):
        qt = _q_group_t(qt_ref, g)
        gsl = slice(g * HEAD_DIM, (g + 1) * HEAD_DIM)
        s_far = _dot(kc_all[:, gsl], qt)
        s_scr[...] = jnp.concatenate([jnp.where(far_ok, sl, NEG) for sl in _head_slabs(s_far)], axis=1)
        s_scr[pl.ds(start, NEAR_CMP), :] = _heads_add(
            _dot(kc_near[:, gsl], qt), [nt_ref[variant, g * GROUP + r] for r in range(GROUP)])
        s = s_scr[...]
        m = jnp.max(s, axis=0, keepdims=True)
        m = jnp.where(m > 0.5 * NEG, m, 0.0)
        e = jnp.exp2(s - m)
        den = jnp.sum(e, axis=0, keepdims=True)
        inv = 1.0 / jnp.where(den > 0, den, 1.0)
        outs += _head_slabs(_dot(vct_ref[gsl, :], e.astype(BF16)) * (inv * _gate_row(gates, 0, g)))
        p = e * inv
        pg = None
        for sl in _head_slabs(p):
            pg = sl if pg is None else pg + sl
        imp = _dot_01_by_f32(ovlt_ref[...], pg)
        score = jnp.where(j_io <= cur, jnp.where(forced, FORCED_SCORE, imp), NEG)
        sels.append(_topk_mask(score, min(N_SELECT, n_blocks), 0))
    o_ref[...] = jnp.concatenate(outs, axis=0).T
    selt_ref[...] = jnp.concatenate(sels, axis=0).astype(BF16)


def _overlap_np(nch, n_blocks, n_cols):
    ci = np.arange(nch)[:, None] * CMP_STRIDE
    sj = np.arange(n_cols)[None, :] * SEL_BLOCK
    ov = (ci < sj + SEL_BLOCK) & (ci + CMP_LEN > sj) & (np.arange(n_cols)[None, :] < n_blocks)
    ov[nch - 1] = False
    return ov.astype(np.float32)


def _cmp_select(qt, kc, vct, near_tiles, gate_t, n_batch, seq):
    nq = seq // Q_BLOCK
    nch = seq // CMP_STRIDE
    n_blocks = seq // SEL_BLOCK
    n = n_batch * seq
    ovlt = jnp.asarray(_overlap_np(nch, n_blocks, n_blocks).T, BF16)
    return pl.pallas_call(
        functools.partial(_cmp_select_kernel, nch=nch, n_blocks=n_blocks),
        grid=(n_batch, nq),
        in_specs=[_qblock_cols(HD, nq),
                  pl.BlockSpec((nch, KVD), lambda b, i: (b, 0)),
                  pl.BlockSpec((None, KVD, nch), lambda b, i: (b, 0, 0)),
                  _const_spec(near_tiles.shape), _const_spec(ovlt.shape), _qblock_cols(LANE, nq)],
        out_specs=[_qblock_rows(HD, nq), _qblock_cols(N_KV * n_blocks, nq)],
        out_shape=[jax.ShapeDtypeStruct((n, HD), F32), jax.ShapeDtypeStruct((N_KV * n_blocks, n), BF16)],
        scratch_shapes=[pltpu.VMEM((nch, GROUP * Q_BLOCK), F32)],
        compiler_params=_cparams("parallel", "parallel"), name="cmp_select")(
            qt, kc, vct, near_tiles, ovlt, gate_t)


FAR_CHUNK = 512


def _selected_kernel(qt_ref, k_ref, vt_ref, selt_ref, ext_ref, bt_ref, gl_ref, o_ref, *, n_blocks):
    i = pl.program_id(1)
    gates = jax.nn.sigmoid(gl_ref[0:N_GATE, :])
    far_end = (i - 1) * Q_BLOCK
    per = FAR_CHUNK // Q_BLOCK
    n_far = (i + per - 2) // per
    qts = [_q_group_t(qt_ref, g) for g in range(N_KV)]
    gsls = [slice(g * HEAD_DIM, (g + 1) * HEAD_DIM) for g in range(N_KV)]

    def mask_bias(g, start, size):
        sel_g = selt_ref[g * n_blocks:(g + 1) * n_blocks, :]
        return (_dot(ext_ref[pl.ds(start, size), :], sel_g) - 1.0) * (-NEG)

    state0 = []
    for g in range(N_KV):
        s_parts, v_parts = [], []
        for j in (1, 0):
            start = pl.multiple_of(jnp.maximum(i - j, 0) * Q_BLOCK, Q_BLOCK)
            mb = mask_bias(g, start, Q_BLOCK)
            if j > 0:
                mb = mb + jnp.where(i >= j, 0.0, NEG)
            sj = _dot(k_ref[pl.ds(start, Q_BLOCK), gsls[g]], qts[g])
            s_parts.append(_heads_add(sj, [bt_ref[j, g * GROUP + r] + mb for r in range(GROUP)]))
            v_parts.append(vt_ref[gsls[g], pl.ds(start, Q_BLOCK)])
        s = jnp.concatenate(s_parts, axis=0)
        m0 = jnp.max(s, axis=0, keepdims=True)
        e = jnp.exp2(s - m0)
        state0 += [m0, jnp.sum(e, axis=0, keepdims=True), _dot(jnp.concatenate(v_parts, axis=1), e.astype(BF16))]

    def far_step(c, carry):
        start = pl.multiple_of(c * FAR_CHUNK, FAR_CHUNK)
        in_far = start + lax.broadcasted_iota(jnp.int32, (FAR_CHUNK, Q_BLOCK), 0) < far_end
        new = []
        for g in range(N_KV):
            m_old, l_old, acc_old = carry[3 * g:3 * g + 3]
            mb = jnp.where(in_far, mask_bias(g, start, FAR_CHUNK), NEG)
            sc = _heads_add(_dot(k_ref[pl.ds(start, FAR_CHUNK), gsls[g]], qts[g]), [mb] * GROUP)
            m_new = jnp.maximum(m_old, jnp.max(sc, axis=0, keepdims=True))
            alpha = jnp.exp2(m_old - m_new)
            p = jnp.exp2(sc - m_new)
            new += [m_new, alpha * l_old + jnp.sum(p, axis=0, keepdims=True),
                    alpha * acc_old + _dot(vt_ref[gsls[g], pl.ds(start, FAR_CHUNK)], p.astype(BF16))]
        return tuple(new)

    final = lax.fori_loop(0, n_far, far_step, tuple(state0))
    outs = []
    for g in range(N_KV):
        _, den, acc = final[3 * g:3 * g + 3]
        outs += _head_slabs(acc * (_gate_row(gates, 1, g) / den))
    o_ref[...] = jnp.concatenate(outs, axis=0).T


def _selected(qt, kvb, vt, v_idx, selt, near_tiles, gate_t, n_batch, seq):
    assert seq % FAR_CHUNK == 0
    nq = seq // Q_BLOCK
    n_blocks = seq // SEL_BLOCK
    ext = jnp.asarray((np.arange(seq)[:, None] // SEL_BLOCK == np.arange(n_blocks)[None, :]).astype(np.float32), BF16)
    return pl.pallas_call(
        functools.partial(_selected_kernel, n_blocks=n_blocks),
        grid=(n_batch, nq),
        in_specs=[_qblock_cols(HD, nq),
                  pl.BlockSpec((seq, KVD), lambda b, i: (b, 2)),
                  pl.BlockSpec((KVD, seq), lambda b, i: (v_idx, b)),
                  _qblock_cols(N_KV * n_blocks, nq), _const_spec(ext.shape), _const_spec(near_tiles.shape),
                  _qblock_cols(LANE, nq)],
        out_specs=_qblock_rows(HD, nq),
        out_shape=jax.ShapeDtypeStruct((n_batch * seq, HD), F32),
        compiler_params=_cparams("parallel", "parallel"), name="selected_attn")(
            qt, kvb, vt, selt, ext, near_tiles, gate_t)


def _block_diag_q(q):
    z = jnp.zeros_like(q)
    row = lax.broadcasted_iota(jnp.int32, (N_HEADS, 2 * HEAD_DIM), 0)
    return jnp.where(row < GROUP, jnp.concatenate([q, z], axis=1), jnp.concatenate([z, q], axis=1))


def _pick_group_half(o_full):
    row = lax.broadcasted_iota(jnp.int32, (N_HEADS, HEAD_DIM), 0)
    return jnp.where(row < GROUP, o_full[:, :HEAD_DIM], o_full[:, HEAD_DIM:])


def _window_decode_kernel(*refs, n_samp, use_sink, use_gate):
    it = iter(refs)
    q_ref, nkv_ref, st_ref, br_ref, b0_ref = next(it), next(it), next(it), next(it), next(it)
    sink_ref = next(it) if use_sink else None
    gl_ref = next(it) if use_gate else None
    o_ref = next(it)
    for s_i in range(n_samp):
        qbd = _block_diag_q(q_ref[s_i])
        kt = st_ref[s_i, 0, 0].astype(BF16)
        vt = st_ref[s_i, 0, 1].astype(BF16)
        new = nkv_ref[s_i]
        s_buf = _dot(qbd, kt) + br_ref[...]
        s_new = jnp.sum(qbd.astype(F32) * new[:, 0:KVD], axis=1, keepdims=True) + b0_ref[...]
        m = jnp.maximum(jnp.max(s_buf, axis=1, keepdims=True), s_new)
        if use_sink:
            m = jnp.maximum(m, sink_ref[...])
        e_buf = jnp.exp(s_buf - m)
        e_new = jnp.exp(s_new - m)
        den = jnp.sum(e_buf, axis=1, keepdims=True) + e_new
        if use_sink:
            den = den + jnp.exp(sink_ref[...] - m)
        o_full = _dot_nt(e_buf.astype(BF16), vt) + e_new * new[:, KVD:2 * KVD]
        f = 1.0 / den
        if use_gate:
            f = f * jax.nn.sigmoid(gl_ref[s_i])
        o_ref[s_i] = _pick_group_half(o_full) * f


def _window_decode(q, new_kv, state, layer, bias_row, bias0, sinks, gate_logits):
    db, lb = state.shape[0], state.shape[-1]
    n_samp = 8 if db % 8 == 0 else 1
    use_sink = sinks is not None
    use_gate = gate_logits is not None
    ins = [q, new_kv.reshape(db, 1, 2 * KVD), state, bias_row, bias0]
    in_specs = [pl.BlockSpec((n_samp, N_HEADS, HEAD_DIM), lambda t: (t, 0, 0)),
                pl.BlockSpec((n_samp, 1, 2 * KVD), lambda t: (t, 0, 0)),
                pl.BlockSpec((n_samp, 1, 2, KVD, lb), lambda t: (t, layer, 0, 0, 0)),
                _const_spec(bias_row.shape), _const_spec(bias0.shape)]
    if use_sink:
        ins.append(sinks.astype(F32).reshape(N_HEADS, 1))
        in_specs.append(_const_spec((N_HEADS, 1)))
    if use_gate:
        ins.append(gate_logits)
        in_specs.append(pl.BlockSpec((n_samp, N_HEADS, 1), lambda t: (t, 0, 0)))
    return pl.pallas_call(
        functools.partial(_window_decode_kernel, n_samp=n_samp, use_sink=use_sink, use_gate=use_gate),
        grid=(db // n_samp,), in_specs=in_specs,
        out_specs=pl.BlockSpec((n_samp, N_HEADS, HEAD_DIM), lambda t: (t, 0, 0)),
        out_shape=jax.ShapeDtypeStruct((db, N_HEADS, HEAD_DIM), F32),
        compiler_params=_cparams("parallel"), name="window_decode")(*ins)


N_KIND = 4
XT_ROWS = 1024


def _nsa_decode_kernel(pt_ref, q_ref, nkv_ref, gl_ref, pool_ref, wc_ref, w1f_ref, pef_ref, w2_ref,
                       cb_ref, sb_ref, b0_ref, ovl_ref, ex_ref, oc_ref, os_ref,
                       buf_a, buf_b, xrow_k, xrow_v, sem, *, layer, n_pages, past, n_blocks, nbp):
    xrow = (xrow_k, xrow_v)
    step = pl.program_id(0)
    n_steps = pl.num_programs(0)
    nch = past // CMP_STRIDE
    n_past = past // SEL_BLOCK

    def page_copy(page, j, buf, slot):
        return pltpu.make_async_copy(
            pool_ref.at[page, layer], buf.at[:, :, pl.ds(j * PAGE_SIZE, PAGE_SIZE)], sem.at[slot])

    def start_fetch(sample, buf, slot):
        def body(j, c):
            page_copy(pt_ref[sample * n_pages + j], j, buf, slot).start()
            return c
        lax.fori_loop(0, n_pages, body, 0)

    def wait_fetch(buf, slot):
        def body(j, c):
            page_copy(0, j, buf, slot).wait()
            return c
        lax.fori_loop(0, n_pages, body, 0)

    def compute(buf, s_i):
        qbd = _block_diag_q(q_ref[s_i])
        new = nkv_ref[s_i]
        gates = jax.nn.sigmoid(gl_ref[s_i])

        def to_rows(t):
            xr = min(XT_ROWS, past)
            for c in range(past // xr):
                xrow[t][c * xr:(c + 1) * xr, :] = buf[t, :, c * xr:(c + 1) * xr].T

        def load_xp(t, p):
            return xrow[t][pl.ds(p, nch, stride=CMP_STRIDE), :]
        kc, vc = _compress(load_xp, wc_ref, w1f_ref, pef_ref, w2_ref, nch, prepare=to_rows)
        s = _dot_nt(qbd, kc.astype(BF16)) + cb_ref[...]
        m = jnp.max(s, axis=1, keepdims=True)
        m = jnp.where(m > 0.5 * NEG, m, 0.0)
        e = jnp.exp(s - m)
        den = jnp.sum(e, axis=1, keepdims=True)
        inv = 1.0 / jnp.where(den > 0, den, 1.0)
        oc_full = _dot(e.astype(BF16), vc.astype(BF16))
        oc_ref[s_i] = _pick_group_half(oc_full) * (inv * gates[0])
        p = e * inv
        j_io = lax.broadcasted_iota(jnp.int32, (1, nbp), 1)
        cur = past // SEL_BLOCK
        forced = (j_io == 0) | (j_io == cur) | (j_io == cur - 1)
        sel_rows = []
        for g in range(N_KV):
            pg = jnp.sum(p[g * GROUP:(g + 1) * GROUP], axis=0, keepdims=True)
            imp = _dot_f32_by_01(pg, ovl_ref[...])
            score = jnp.where(j_io <= cur, jnp.where(forced, FORCED_SCORE, imp), NEG)
            sel_rows.append(_topk_mask_row(score, min(N_SELECT, n_blocks))[:, 0:n_past])
        row = lax.broadcasted_iota(jnp.int32, (N_HEADS, n_past), 0)
        sel16 = jnp.where(row < GROUP, sel_rows[0], sel_rows[1]).astype(BF16)
        selm = _dot(sel16, ex_ref[...])
        s2 = _dot(qbd, buf[2].astype(BF16)) + sb_ref[...] + (selm - 1.0) * (-NEG)
        s_new = jnp.sum(qbd.astype(F32) * new[:, 0:KVD], axis=1, keepdims=True) + b0_ref[...]
        m2 = jnp.maximum(jnp.max(s2, axis=1, keepdims=True), s_new)
        e2 = jnp.exp(s2 - m2)
        e_new = jnp.exp(s_new - m2)
        den2 = jnp.sum(e2, axis=1, keepdims=True) + e_new
        os_full = _dot_nt(e2.astype(BF16), buf[3].astype(BF16)) + e_new * new[:, KVD:2 * KVD]
        os_ref[s_i] = _pick_group_half(os_full) * (gates[1] / den2)

    @pl.when(step == 0)
    def _():
        start_fetch(0, buf_a, 0)

    start_fetch(2 * step + 1, buf_b, 1)
    wait_fetch(buf_a, 0)
    compute(buf_a, 0)

    @pl.when(step + 1 < n_steps)
    def _():
        start_fetch(2 * step + 2, buf_a, 0)

    wait_fetch(buf_b, 1)
    compute(buf_b, 1)


def _nsa_decode(q, new_sel, gate_logits2, pool, page_table, layer, cw, cmp_bias, sel_bias, bias0):
    db, n_pages = page_table.shape
    assert db % 2 == 0
    past = n_pages * PAGE_SIZE
    nch = past // CMP_STRIDE
    n_past = past // SEL_BLOCK
    n_blocks = n_past + 1
    nbp = -(-n_blocks // LANE) * LANE
    wc, w1f, pef, w2bd = cw
    ovl = jnp.asarray(_overlap_np(nch, n_blocks, nbp), BF16)
    ex = jnp.asarray((np.arange(n_past)[:, None] == (np.arange(past)[None, :] // SEL_BLOCK)).astype(np.float32), BF16)
    consts = [wc, w1f, pef, w2bd, cmp_bias, sel_bias, bias0, ovl, ex]
    per2 = lambda *tail: pl.BlockSpec((2,) + tail, lambda t, pt: (t,) + (0,) * len(tail))
    grid_spec = pltpu.PrefetchScalarGridSpec(
        num_scalar_prefetch=1, grid=(db // 2,),
        in_specs=[per2(N_HEADS, HEAD_DIM), per2(1, 2 * KVD), per2(2, N_HEADS, 1),
                  pl.BlockSpec(memory_space=pl.ANY)] + [_const_spec(c.shape) for c in consts],
        out_specs=[per2(N_HEADS, HEAD_DIM), per2(N_HEADS, HEAD_DIM)],
        scratch_shapes=[pltpu.VMEM((N_KIND, KVD, past), F32), pltpu.VMEM((N_KIND, KVD, past), F32),
                        pltpu.VMEM((past, KVD), F32), pltpu.VMEM((past, KVD), F32), pltpu.SemaphoreType.DMA((2,))])
    out = jax.ShapeDtypeStruct((db, N_HEADS, HEAD_DIM), F32)
    return pl.pallas_call(
        functools.partial(_nsa_decode_kernel, layer=layer, n_pages=n_pages, past=past,
                          n_blocks=n_blocks, nbp=nbp),
        grid_spec=grid_spec, out_shape=[out, out],
        compiler_params=_cparams("arbitrary"), name="nsa_decode")(
            page_table.reshape(-1), q, new_sel.reshape(db, 1, 2 * KVD), gate_logits2, pool, *consts)


def _shift_kernel(st_ref, new_ref, o_ref, *, lb):
    x = st_ref[...]
    lane = lax.broadcasted_iota(jnp.int32, x.shape, 2)
    o_ref[...] = jnp.where(lane == lb - 1, new_ref[...], pltpu.roll(x, lb - 1, 2))


def _shift_state(state, new_rows):
    db, w, lb = state.shape
    ns = max(1, min(db, (4 << 20) // (w * lb * 4)))
    assert db % ns == 0
    blk = pl.BlockSpec((ns, w, lb), lambda t: (t, 0, 0))
    return pl.pallas_call(
        functools.partial(_shift_kernel, lb=lb), grid=(db // ns,),
        in_specs=[blk, pl.BlockSpec((ns, w, 1), lambda t: (t, 0, 0))], out_specs=blk,
        out_shape=jax.ShapeDtypeStruct((db, w, lb), state.dtype),
        compiler_params=_cparams("parallel"), name="shift_state")(state, new_rows.reshape(db, w, 1))


def _decode_bias_row(rel_bias, past, kpos, window):
    dist = past - kpos
    valid = (dist >= 0) if window is None else ((dist >= 0) & (dist < window))
    return _bias_tile(rel_bias, dist[None, :], valid[None, :], False)[:, 0, :]


def kernel(x_prompt, x_sample, state_swa, cache_nsa, state_nsa_win, page_table, rel_bias, norm_mix, norm_mlp,
           norm_final, w_in_a, b_in_a, w_out_a, b_out_a, sinks_a, w_in_b, w_out_b, w_cmp1, w_cmp2, pe_cmp,
           w_up, w_down):
    n_batch, seq, _ = x_prompt.shape
    db = x_sample.shape[0]
    assert x_sample.shape[1] == 1
    depth = norm_mix.shape[0]
    n_la, n_lb = w_in_a.shape[0], w_in_b.shape[0]
    lba, lbb = state_swa.shape[1], state_nsa_win.shape[1]
    n_pages = page_table.shape[1]
    past = n_pages * PAGE_SIZE
    assert seq >= WIN_B and lba == WIN_A and lbb == WIN_B and past >= WIN_B

    kk = np.arange(Q_BLOCK)[:, None]
    qq = np.arange(Q_BLOCK)[None, :]
    d0, d1 = qq - kk, Q_BLOCK + qq - kk
    swa_tiles = LOG2E * jnp.stack([_bias_tile(rel_bias, d0, (d0 >= 0) & (d0 < WIN_A), False),
                                   _bias_tile(rel_bias, d1, (d1 >= 0) & (d1 < WIN_A), False)])
    near_tiles = LOG2E * jnp.stack([_bias_tile(rel_bias, d0, d0 >= 0, True),
                                    _bias_tile(rel_bias, d1, d1 >= 0, True)])
    nback_b = WIN_B // Q_BLOCK
    far_mask = jnp.asarray(np.where(kk > qq, 0.0, NEG), F32)
    no_mask = jnp.zeros((8, LANE), F32)
    cc = np.arange(NEAR_CMP)[:, None]
    dc0 = qq - CMP_STRIDE * cc - (CMP_LEN - 1)
    dc1 = qq - CMP_STRIDE * (cc - NEAR_CMP // 2) - (CMP_LEN - 1)
    cmp_tiles = LOG2E * jnp.stack([_bias_tile(rel_bias, dc0, dc0 >= 0, True),
                                   _bias_tile(rel_bias, dc1, dc1 >= 0, True)])
    nch_d = past // CMP_STRIDE
    cmp_row = _decode_bias_row(rel_bias, past, np.arange(nch_d) * CMP_STRIDE + CMP_LEN - 1, None)
    sel_row = _decode_bias_row(rel_bias, past, np.arange(past), None)
    swa_row = _decode_bias_row(rel_bias, past, past - lba + np.arange(lba), WIN_A)
    win_row = _decode_bias_row(rel_bias, past, past - lbb + np.arange(lbb), WIN_B)
    bias0 = rel_bias.astype(F32)[0].reshape(N_HEADS, 1)

    xp = x_prompt.reshape(n_batch * seq, D_MODEL)
    xs = x_sample.reshape(db, D_MODEL)
    to_minor = lambda a: jnp.transpose(a, (0, 2, 3, 4, 5, 1))
    st_swa = to_minor(state_swa).reshape(db, n_la, 2, KVD, lba)
    st_win = to_minor(state_nsa_win).reshape(db, n_lb, 2, KVD, lbb)
    pool = to_minor(cache_nsa).reshape(cache_nsa.shape[0], n_lb, N_KIND, KVD, PAGE_SIZE)
    swa_p, swa_new, rows_s, win_p, win_new = [], [], [], [], []
    rows_pt = jnp.zeros((n_batch, n_lb, N_KIND * KVD, seq), F32)

    for layer in range(depth):
        final = norm_final if layer == depth - 1 else None
        if layer % 2 == 0:
            a = layer // 2
            qt_p, kv_p, kvb_p, vt_p = _project(xp, norm_mix[layer], w_in_a[a], b_in_a[a], 2 * KVD, False, (KVD,))
            q_s, kv_s, _ = _project(xs, norm_mix[layer], w_in_a[a], b_in_a[a], 2 * KVD, False, None)
            o_p = _banded(qt_p, kvb_p, 0, vt_p, 0, n_batch, seq, ("bias", "bias"), swa_tiles, no_mask,
                          sinks_a[a], None, None)
            o_s = _window_decode(jnp.swapaxes(q_s, 0, 1), kv_s, st_swa, a, swa_row, bias0, sinks_a[a], None)
            swa_p.append(kv_p.reshape(n_batch, seq, 2, N_KV, HEAD_DIM)[:, seq - WIN_A:])
            swa_new.append(kv_s)
            xp = _out_mlp(xp, [o_p], w_out_a[a], b_out_a[a], norm_mlp[layer], w_up[layer], w_down[layer], final)
            xs = _out_mlp(xs, [o_s.reshape(db, HD)], w_out_a[a], b_out_a[a], norm_mlp[layer], w_up[layer],
                          w_down[layer], final)
        else:
            b = layer // 2
            cw = _compress_weights(w_cmp1[b], w_cmp2[b], pe_cmp[b])
            w_in = jnp.pad(w_in_b[b], ((0, 0), (0, LANE - N_GATE)))
            qt_p, kv_p, kvb_p, vt_p, glt_p, rows_pt = _project(xp, norm_mix[layer], w_in, None, 6 * KVD, True,
                                                               (3 * KVD, 5 * KVD), (rows_pt, b))
            q_s, kv_s, _, gl_s = _project(xs, norm_mix[layer], w_in, None, 6 * KVD, True, None)
            kc, vct = _compress_prompt(kv_p, n_batch, seq, cw)
            o_c, selt = _cmp_select(qt_p, kc, vct, cmp_tiles, glt_p, n_batch, seq)
            o_sel = _selected(qt_p, kvb_p, vt_p, 0, selt, near_tiles, glt_p, n_batch, seq)
            kinds = ("bias", "bias") + ("none",) * (nback_b - 2) + ("mask",)
            o_w = _banded(qt_p, kvb_p, 4, vt_p, 1, n_batch, seq, kinds, near_tiles, far_mask, None, glt_p, 2)
            win_p.append(kv_p.reshape(n_batch, seq, 6 * KVD)[:, seq - WIN_B:, 4 * KVD:]
                         .reshape(n_batch, WIN_B, 2, N_KV, HEAD_DIM))
            xp = _out_mlp(xp, [o_c, o_sel, o_w], w_out_b[b], None, norm_mlp[layer], w_up[layer], w_down[layer], final)
            q_sd = jnp.swapaxes(q_s, 0, 1)
            gl_s3 = gl_s.reshape(db, 3, N_HEADS, 1)
            oc_s, os_s = _nsa_decode(q_sd, kv_s[:, 2 * KVD:4 * KVD], gl_s3[:, 0:2], pool, page_table, b, cw,
                                     cmp_row, sel_row, bias0)
            ow_s = _window_decode(q_sd, kv_s[:, 4 * KVD:], st_win, b, win_row, bias0, None, gl_s3[:, 2])
            rows_s.append(kv_s[:, :4 * KVD].reshape(db, 1, 4, N_KV, HEAD_DIM))
            win_new.append(kv_s[:, 4 * KVD:])
            xs = _out_mlp(xs, [oc_s.reshape(db, HD), os_s.reshape(db, HD), ow_s.reshape(db, HD)], w_out_b[b], None,
                          norm_mlp[layer], w_up[layer], w_down[layer], final)

    y_prompt = xp.reshape(n_batch, seq, D_MODEL)
    y_sample = xs.reshape(db, 1, D_MODEL)
    from_minor = lambda a, nl, lb: jnp.transpose(a.reshape(db, nl, 2, N_KV, HEAD_DIM, lb), (0, 5, 1, 2, 3, 4))
    swa_sample = _shift_state(st_swa.reshape(db, n_la * 2 * KVD, lba), jnp.concatenate(swa_new, axis=1))
    win_sample = _shift_state(st_win.reshape(db, n_lb * 2 * KVD, lbb), jnp.concatenate(win_new, axis=1))
    return (y_prompt, y_sample,
            jnp.stack(swa_p, axis=2),
            from_minor(swa_sample, n_la, lba),
            jnp.transpose(rows_pt.reshape(n_batch, n_lb, N_KIND, N_KV, HEAD_DIM, seq), (0, 5, 1, 2, 3, 4)),
            jnp.stack(rows_s, axis=2),
            jnp.stack(win_p, axis=2),
            from_minor(win_sample, n_lb, lbb))
```

```python
import functools
import math

import numpy as np
import jax
import jax.numpy as jnp
from jax import lax
from jax.experimental import pallas as pl
from jax.experimental.pallas import tpu as pltpu

D_MODEL = 1024
N_HEADS = 16
HEAD_DIM = 64
N_KV = 2
GROUP = N_HEADS // N_KV
HD = N_HEADS * HEAD_DIM
KVD = N_KV * HEAD_DIM
D_FF = 4 * D_MODEL
PAGE_SIZE = 128
WIN_A = 128
WIN_B = 512
Q_BLOCK = 128
CMP_STRIDE = 16
CMP_LEN = 2 * CMP_STRIDE
CMP_HIDDEN = 128
SEL_BLOCK = 64
N_SELECT = 16
FORCED_SCORE = 1e9
N_BUCKETS = 32
MAX_DISTANCE = 128
NORM_EPS = 1e-5
SCALE = HEAD_DIM ** -0.5
LOG2E = math.log2(math.e)
N_GATE = 3 * N_HEADS
LANE = 128

NEG = -1e30
REMOVED = -2e30
F32 = jnp.float32
BF16 = jnp.bfloat16
VMEM_LIMIT = 56 * 1024 * 1024


def _cparams(*sem):
    return pltpu.CompilerParams(dimension_semantics=sem, vmem_limit_bytes=VMEM_LIMIT)


def _const_spec(shape):
    nd = len(shape)
    return pl.BlockSpec(shape, lambda *_: (0,) * nd, pipeline_mode=pl.Buffered(1))


def _bucket_np(dist):
    n = np.maximum(dist, 0)
    exact = N_BUCKETS // 2
    nf = np.maximum(n, exact).astype(np.float32)
    large = exact + (np.log(nf / np.float32(exact)) / np.float32(math.log(MAX_DISTANCE / exact))
                     * np.float32(N_BUCKETS - exact)).astype(np.int32)
    return np.where(n < exact, n, np.minimum(large, N_BUCKETS - 1)).astype(np.int32)


def _bias_tile(rel_bias, dist, valid, shift):
    tb = rel_bias.astype(F32)
    if shift:
        tb = tb - tb[N_BUCKETS - 1][None, :]
    bucket = jnp.asarray(_bucket_np(dist).reshape(1, -1))
    onehot = (bucket == jnp.arange(N_BUCKETS, dtype=jnp.int32)[:, None]).astype(F32)
    b = jnp.dot(tb.T, onehot, precision=lax.Precision.HIGHEST).reshape((N_HEADS,) + dist.shape)
    return jnp.where(jnp.asarray(valid)[None], b, NEG)


def _rms(x, g):
    ms = jnp.mean(x * x, axis=-1, keepdims=True)
    return x * lax.rsqrt(ms + NORM_EPS) * g


def _dot(a, b):
    return jnp.dot(a, b, preferred_element_type=F32)


def _dot_nt(a, b):
    return lax.dot_general(a, b, (((1,), (1,)), ((), ())), preferred_element_type=F32)


def _split3(a):
    hi = a.astype(BF16)
    r1 = a - hi.astype(F32)
    mid = r1.astype(BF16)
    lo = (r1 - mid.astype(F32)).astype(BF16)
    return hi, mid, lo


def _dot_f32_by_01(a, b01):
    hi, mid, lo = _split3(a)
    return _dot(hi, b01) + _dot(mid, b01) + _dot(lo, b01)


def _dot_01_by_f32(b01, a):
    hi, mid, lo = _split3(a)
    return _dot(b01, hi) + _dot(b01, mid) + _dot(b01, lo)


def _topk_mask(score, k, axis):
    n = score.shape[axis]
    iota = lax.broadcasted_iota(jnp.int32, score.shape, axis).astype(F32)
    sel = jnp.zeros_like(score)
    s = score
    for _ in range(k):
        m = jnp.max(s, axis=axis, keepdims=True)
        first = jnp.min(jnp.where(s == m, iota, float(n)), axis=axis, keepdims=True)
        hit = iota == first
        ok = jnp.where(m > 0.5 * NEG, 1.0, 0.0)
        sel = jnp.where(hit, ok, sel)
        s = jnp.where(hit, REMOVED, s)
    return sel


def _topk_mask_row(score, k):
    n = score.shape[1]
    s_j = jnp.broadcast_to(score, (n, n))
    s_i = jnp.concatenate([jnp.broadcast_to(score, (LANE, n)).T] * (n // LANE), axis=1)
    i_io = lax.broadcasted_iota(jnp.int32, (n, n), 0)
    j_io = lax.broadcasted_iota(jnp.int32, (n, n), 1)
    ahead = jnp.where(s_i > s_j, 1.0, jnp.where(s_i == s_j, jnp.where(i_io < j_io, 1.0, 0.0), 0.0))
    rank = jnp.sum(ahead, axis=0, keepdims=True)
    return jnp.where(rank < k, jnp.where(score > 0.5 * NEG, 1.0, 0.0), 0.0)


def _q_group_t(qt_ref, g):
    return jnp.concatenate(
        [qt_ref[(g * GROUP + r) * HEAD_DIM:(g * GROUP + r + 1) * HEAD_DIM, :] for r in range(GROUP)], axis=1)


def _heads_add(s, tiles):
    return jnp.concatenate([s[:, r * Q_BLOCK:(r + 1) * Q_BLOCK] + tiles[r] for r in range(GROUP)], axis=1)


def _head_row(rows):
    return jnp.concatenate(rows, axis=1)


def _head_slabs(acc_t):
    return [acc_t[:, r * Q_BLOCK:(r + 1) * Q_BLOCK] for r in range(GROUP)]


def _proj_kernel(*refs, n_kv, has_bias, has_gate, v_cols, n_rows):
    it = iter(refs)
    x_ref, g_ref, w_ref = next(it), next(it), next(it)
    b_ref = next(it) if has_bias else None
    if n_rows:
        next(it)
    q_ref, kv_ref, kvb_ref = next(it), next(it), next(it)
    vt_ref = next(it) if v_cols is not None else None
    gl_ref = next(it) if has_gate else None
    rows_ref = next(it) if n_rows else None
    xn = _rms(x_ref[...], g_ref[...]).astype(BF16)
    u = _dot(xn, w_ref[...])
    if has_bias:
        u = u + b_ref[...]
    kv = u[:, HD:HD + n_kv]
    kv_ref[...] = kv
    kvb_ref[...] = kv.astype(BF16)
    if v_cols is None:
        for h in range(N_HEADS):
            q_ref[h] = (u[:, h * HEAD_DIM:(h + 1) * HEAD_DIM] * SCALE).astype(BF16)
        if has_gate:
            gl_ref[...] = u[:, HD + n_kv:HD + n_kv + N_GATE]
    else:
        q_ref[...] = (u[:, :HD] * (SCALE * LOG2E)).T.astype(BF16)
        for idx, c0 in enumerate(v_cols):
            vt_ref[idx * KVD:(idx + 1) * KVD, :] = u[:, HD + c0:HD + c0 + KVD].T.astype(BF16)
        if has_gate:
            gl_ref[...] = u[:, HD + n_kv:HD + n_kv + LANE].T
        if n_rows:
            rows_ref[...] = u[:, HD:HD + n_rows].T


def _project(x, g, w, b, n_kv, has_gate, v_cols, rows_into=None):
    n = x.shape[0]
    dout = w.shape[1]
    tm = min(512, n)
    assert n % tm == 0
    has_bias = b is not None
    prompt = v_cols is not None
    ins = [x, g.reshape(1, D_MODEL), w.astype(BF16)]
    in_specs = [pl.BlockSpec((tm, D_MODEL), lambda t: (t, 0)),
                _const_spec((1, D_MODEL)), _const_spec((D_MODEL, dout))]
    if has_bias:
        ins.append(b.reshape(1, dout))
        in_specs.append(_const_spec((1, dout)))
    aliases = {}
    n_rows = 0
    if rows_into is not None:
        rows_arr, rows_layer = rows_into
        n_rows, seq = rows_arr.shape[2], rows_arr.shape[3]
        assert seq % tm == 0
        aliases = {len(ins): 0}
        ins.append(rows_arr)
        in_specs.append(pl.BlockSpec(memory_space=pl.ANY))
    rows = lambda wd, dt: (jax.ShapeDtypeStruct((n, wd), dt), pl.BlockSpec((tm, wd), lambda t: (t, 0)))
    cols = lambda ht, dt: (jax.ShapeDtypeStruct((ht, n), dt), pl.BlockSpec((ht, tm), lambda t: (0, t)))
    if prompt:
        outs = [cols(HD, BF16), rows(n_kv, F32), rows(n_kv, BF16), cols(len(v_cols) * KVD, BF16)]
        if has_gate:
            outs.append(cols(LANE, F32))
    else:
        outs = [(jax.ShapeDtypeStruct((N_HEADS, n, HEAD_DIM), BF16),
                 pl.BlockSpec((N_HEADS, tm, HEAD_DIM), lambda t: (0, t, 0))),
                rows(n_kv, F32), rows(n_kv, BF16)]
        if has_gate:
            outs.append(rows(N_GATE, F32))
    if n_rows:
        spb = seq // tm
        aliases = {k: len(outs) for k in aliases}
        outs.append((jax.ShapeDtypeStruct(rows_arr.shape, rows_arr.dtype),
                     pl.BlockSpec((None, None, n_rows, tm), lambda t: (t // spb, rows_layer, 0, t % spb))))
    return pl.pallas_call(
        functools.partial(_proj_kernel, n_kv=n_kv, has_bias=has_bias, has_gate=has_gate, v_cols=v_cols,
                          n_rows=n_rows),
        grid=(n // tm,), in_specs=in_specs, out_specs=[o[1] for o in outs], out_shape=[o[0] for o in outs],
        input_output_aliases=aliases,
        compiler_params=_cparams("parallel"), name="norm_proj")(*ins)


FF_CHUNK = 1024


def _out_mlp_kernel(*refs, n_o, has_bias, final):
    it = iter(refs)
    x_ref = next(it)
    o_refs = [next(it) for _ in range(n_o)]
    wo_ref = next(it)
    bo_ref = next(it) if has_bias else None
    gm_ref, wup_ref, wdn_ref = next(it), next(it), next(it)
    gf_ref = next(it) if final else None
    y_ref = next(it)
    o = o_refs[0][...]
    for r in o_refs[1:]:
        o = o + r[...]
    x1 = x_ref[...] + _dot(o.astype(BF16), wo_ref[...])
    if has_bias:
        x1 = x1 + bo_ref[...]
    xn = _rms(x1, gm_ref[...]).astype(BF16)
    acc = x1
    for c in range(D_FF // FF_CHUNK):
        h = jnp.maximum(_dot(xn, wup_ref[:, c * FF_CHUNK:(c + 1) * FF_CHUNK]), 0.0)
        acc = acc + _dot((h * h).astype(BF16), wdn_ref[c * FF_CHUNK:(c + 1) * FF_CHUNK, :])
    y_ref[...] = _rms(acc, gf_ref[...]) if final else acc


def _out_mlp(x, os_, w_out, b_out, g_mlp, w_up, w_down, g_final):
    n = x.shape[0]
    tm = min(512, n)
    assert n % tm == 0
    has_bias = b_out is not None
    final = g_final is not None
    row = pl.BlockSpec((tm, D_MODEL), lambda t: (t, 0))
    ins = [x, *os_, w_out.astype(BF16)]
    in_specs = [row] + [row] * len(os_) + [_const_spec((HD, D_MODEL))]
    if has_bias:
        ins.append(b_out.reshape(1, D_MODEL))
        in_specs.append(_const_spec((1, D_MODEL)))
    ins += [g_mlp.reshape(1, D_MODEL), w_up.astype(BF16), w_down.astype(BF16)]
    in_specs += [_const_spec((1, D_MODEL)), _const_spec((D_MODEL, D_FF)), _const_spec((D_FF, D_MODEL))]
    if final:
        ins.append(g_final.reshape(1, D_MODEL))
        in_specs.append(_const_spec((1, D_MODEL)))
    return pl.pallas_call(
        functools.partial(_out_mlp_kernel, n_o=len(os_), has_bias=has_bias, final=final),
        grid=(n // tm,), in_specs=in_specs, out_specs=row,
        out_shape=jax.ShapeDtypeStruct((n, D_MODEL), F32),
        compiler_params=_cparams("parallel"), name="out_mlp")(*ins)


def _qblock_cols(height, nq):
    return pl.BlockSpec((height, Q_BLOCK), lambda b, i: (0, b * nq + i))


def _qblock_rows(width, nq):
    return pl.BlockSpec((Q_BLOCK, width), lambda b, i: (b * nq + i, 0))


def _gate_row(gates, branch, g):
    return _head_row([gates[branch * N_HEADS + g * GROUP + r:branch * N_HEADS + g * GROUP + r + 1, :]
                      for r in range(GROUP)])


def _banded_kernel(*refs, kinds, use_sink, gate_branch):
    it = iter(refs)
    qt_ref, k_ref, vt_ref, bt_ref, mt_ref = next(it), next(it), next(it), next(it), next(it)
    sink_ref = next(it) if use_sink else None
    gl_ref = next(it) if gate_branch is not None else None
    o_ref = next(it)
    i = pl.program_id(1)
    gates = jax.nn.sigmoid(gl_ref[0:N_GATE, :]) if gate_branch is not None else None
    bias_slot = {}
    for j, kind in enumerate(kinds):
        if kind == "bias":
            bias_slot[j] = len(bias_slot)
    scores = []
    for g in range(N_KV):
        qt = _q_group_t(qt_ref, g)
        gsl = slice(g * HEAD_DIM, (g + 1) * HEAD_DIM)
        s_parts, v_parts = [], []
        for j, kind in enumerate(kinds):
            start = pl.multiple_of(jnp.maximum(i - j, 0) * Q_BLOCK, Q_BLOCK)
            v_parts.append(vt_ref[gsl, pl.ds(start, Q_BLOCK)])
            sj = _dot(k_ref[pl.ds(start, Q_BLOCK), gsl], qt)
            pen = jnp.where(i >= j, 0.0, NEG) if j > 0 else 0.0
            if kind == "bias":
                sj = _heads_add(sj, [bt_ref[bias_slot[j], g * GROUP + r] + pen for r in range(GROUP)])
            elif kind == "mask":
                sj = _heads_add(sj, [mt_ref[...] + pen] * GROUP)
            else:
                sj = sj + pen
            s_parts.append(sj)
        scores.append((jnp.concatenate(s_parts, axis=0), jnp.concatenate(v_parts, axis=1)))
    outs = []
    for g, (s, vt) in enumerate(scores):
        m = jnp.max(s, axis=0, keepdims=True)
        if use_sink:
            sk = _head_row([jnp.full((1, Q_BLOCK), sink_ref[g * GROUP + r] * LOG2E, F32) for r in range(GROUP)])
            m = jnp.maximum(m, sk)
        e = jnp.exp2(s - m)
        den = jnp.sum(e, axis=0, keepdims=True)
        if use_sink:
            den = den + jnp.exp2(sk - m)
        f = 1.0 / den
        if gates is not None:
            f = f * _gate_row(gates, gate_branch, g)
        outs += _head_slabs(_dot(vt, e.astype(BF16)) * f)
    o_ref[...] = jnp.concatenate(outs, axis=0).T


def _banded(qt, kvb, k_col, vt, v_idx, n_batch, seq, kinds, bias_tiles, mask_tile, sinks, gate_t, gate_branch):
    nq = seq // Q_BLOCK
    use_sink = sinks is not None
    ins = [qt, kvb, vt, bias_tiles, mask_tile]
    in_specs = [_qblock_cols(HD, nq),
                pl.BlockSpec((seq, KVD), lambda b, i: (b, k_col)),
                pl.BlockSpec((KVD, seq), lambda b, i: (v_idx, b)),
                _const_spec(bias_tiles.shape), _const_spec(mask_tile.shape)]
    if use_sink:
        ins.append(sinks.astype(F32))
        in_specs.append(pl.BlockSpec(memory_space=pltpu.SMEM))
    if gate_t is not None:
        ins.append(gate_t)
        in_specs.append(_qblock_cols(LANE, nq))
    else:
        gate_branch = None
    return pl.pallas_call(
        functools.partial(_banded_kernel, kinds=kinds, use_sink=use_sink, gate_branch=gate_branch),
        grid=(n_batch, nq), in_specs=in_specs, out_specs=_qblock_rows(HD, nq),
        out_shape=jax.ShapeDtypeStruct((n_batch * seq, HD), F32),
        compiler_params=_cparams("parallel", "parallel"), name="banded_attn")(*ins)


def _compress(load_xp, wc_ref, w1f_ref, pef_ref, w2_ref, nch, prepare=None):
    outs = []
    if prepare is not None:
        for t in range(2):
            prepare(t)
    for t in range(2):
        xs = jnp.concatenate([load_xp(t, p).astype(BF16) for p in range(CMP_STRIDE)], axis=1)
        acc = _dot(xs, wc_ref[t])
        pet = _dot(pef_ref[t], w1f_ref[t])
        lo = acc[:, :2 * CMP_HIDDEN]
        hi_next = pltpu.roll(acc[:, 2 * CMP_HIDDEN:], nch - 1, 0)
        hh = lo + hi_next + jnp.concatenate([pet, pet], axis=1)
        a = (hh * jax.nn.sigmoid(hh)).astype(BF16)
        outs.append(_dot(a, w2_ref[t]))
    return outs


def _compress_weights(w1, w2, pe):
    z = jnp.zeros((2, CMP_STRIDE, HEAD_DIM, CMP_HIDDEN), w1.dtype)
    lo, hi = w1[:, :CMP_STRIDE], w1[:, CMP_STRIDE:]
    top = jnp.concatenate([lo, z, hi, z], axis=-1)
    bot = jnp.concatenate([z, lo, z, hi], axis=-1)
    wp = jnp.concatenate([top, bot], axis=2)
    wc = wp.reshape(2, CMP_STRIDE * KVD, 4 * CMP_HIDDEN).astype(BF16)
    z2 = jnp.zeros((2, CMP_HIDDEN, HEAD_DIM), w2.dtype)
    w2bd = jnp.concatenate([jnp.concatenate([w2, z2], axis=-1), jnp.concatenate([z2, w2], axis=-1)], axis=1)
    w1f = w1.reshape(2, CMP_LEN * HEAD_DIM, CMP_HIDDEN).astype(BF16)
    pef = pe.reshape(2, 1, CMP_LEN * HEAD_DIM).astype(BF16)
    return wc, w1f, pef, w2bd.astype(BF16)


def _compress_prompt_kernel(x_ref, wc_ref, w1f_ref, pef_ref, w2_ref, kc_ref, vct_ref, *, nch):
    def load_xp(t, p):
        return x_ref[:, p, t * KVD:(t + 1) * KVD]
    kc, vc = _compress(load_xp, wc_ref, w1f_ref, pef_ref, w2_ref, nch)
    kc_ref[...] = kc
    vct_ref[...] = vc.T.astype(BF16)


def _compress_prompt(kvf, n_batch, seq, cw):
    nch = seq // CMP_STRIDE
    wc, w1f, pef, w2bd = cw
    return pl.pallas_call(
        functools.partial(_compress_prompt_kernel, nch=nch),
        grid=(n_batch,),
        in_specs=[pl.BlockSpec((nch, CMP_STRIDE, 2 * KVD), lambda b: (b, 0, 0)),
                  _const_spec(wc.shape), _const_spec(w1f.shape), _const_spec(pef.shape), _const_spec(w2bd.shape)],
        out_specs=[pl.BlockSpec((nch, KVD), lambda b: (b, 0)), pl.BlockSpec((None, KVD, nch), lambda b: (b, 0, 0))],
        out_shape=[jax.ShapeDtypeStruct((n_batch * nch, KVD), F32), jax.ShapeDtypeStruct((n_batch, KVD, nch), BF16)],
        compiler_params=_cparams("parallel"), name="compress_prompt")(
            kvf.reshape(n_batch * nch, CMP_STRIDE, kvf.shape[1]), wc, w1f, pef, w2bd)


NEAR_CMP = 16


def _cmp_select_kernel(qt_ref, kc_ref, vct_ref, nt_ref, ovlt_ref, gl_ref, o_ref, selt_ref, s_scr, *, nch, n_blocks):
    i = pl.program_id(1)
    gates = jax.nn.sigmoid(gl_ref[0:N_GATE, :])
    cs = i * (Q_BLOCK // CMP_STRIDE) - NEAR_CMP // 2
    start = pl.multiple_of(jnp.maximum(cs, 0), 8)
    variant = jnp.minimum(i, 1)
    far_ok = lax.broadcasted_iota(jnp.int32, (nch, Q_BLOCK), 0) < cs
    kc_all = kc_ref[...].astype(BF16)
    kc_near = kc_ref[pl.ds(start, NEAR_CMP), :].astype(BF16)
    j_io = lax.broadcasted_iota(jnp.int32, (n_blocks, Q_BLOCK), 0)
    q_io = lax.broadcasted_iota(jnp.int32, (n_blocks, Q_BLOCK), 1)
    cur = (i * Q_BLOCK + q_io) // SEL_BLOCK
    forced = (j_io == 0) | (j_io == cur) | (j_io == cur - 1)
    for g in range(N_KV):
        qt = _q_group_t(qt_ref, g)
        gsl = slice(g * HEAD_DIM, (g + 1) * HEAD_DIM)
        s_far = _dot(kc_all[:, gsl], qt)
        s_scr[g] = jnp.concatenate([jnp.where(far_ok, sl, NEG) for sl in _head_slabs(s_far)], axis=1)
        s_scr[g, pl.ds(start, NEAR_CMP), :] = _heads_add(
            _dot(kc_near[:, gsl], qt), [nt_ref[variant, g * GROUP + r] for r in range(GROUP)])
    outs, sels = [], []
    for g in range(N_KV):
        gsl = slice(g * HEAD_DIM, (g + 1) * HEAD_DIM)
        s = s_scr[g]
        m = jnp.max(s, axis=0, keepdims=True)
        m = jnp.where(m > 0.5 * NEG, m, 0.0)
        e = jnp.exp2(s - m)
        den = jnp.sum(e, axis=0, keepdims=True)
        inv = 1.0 / jnp.where(den > 0, den, 1.0)
        outs += _head_slabs(_dot(vct_ref[gsl, :], e.astype(BF16)) * (inv * _gate_row(gates, 0, g)))
        p = e * inv
        pg = None
        for sl in _head_slabs(p):
            pg = sl if pg is None else pg + sl
        imp = _dot_01_by_f32(ovlt_ref[...], pg)
        score = jnp.where(j_io <= cur, jnp.where(forced, FORCED_SCORE, imp), NEG)
        sels.append(_topk_mask(score, min(N_SELECT, n_blocks), 0))
    o_ref[...] = jnp.concatenate(outs, axis=0).T
    selt_ref[...] = jnp.concatenate(sels, axis=0).astype(BF16)


def _overlap_np(nch, n_blocks, n_cols):
    ci = np.arange(nch)[:, None] * CMP_STRIDE
    sj = np.arange(n_cols)[None, :] * SEL_BLOCK
    ov = (ci < sj + SEL_BLOCK) & (ci + CMP_LEN > sj) & (np.arange(n_cols)[None, :] < n_blocks)
    ov[nch - 1] = False
    return ov.astype(np.float32)


def _cmp_select(qt, kc, vct, near_tiles, gate_t, n_batch, seq):
    nq = seq // Q_BLOCK
    nch = seq // CMP_STRIDE
    n_blocks = seq // SEL_BLOCK
    n = n_batch * seq
    ovlt = jnp.asarray(_overlap_np(nch, n_blocks, n_blocks).T, BF16)
    return pl.pallas_call(
        functools.partial(_cmp_select_kernel, nch=nch, n_blocks=n_blocks),
        grid=(n_batch, nq),
        in_specs=[_qblock_cols(HD, nq),
                  pl.BlockSpec((nch, KVD), lambda b, i: (b, 0)),
                  pl.BlockSpec((None, KVD, nch), lambda b, i: (b, 0, 0)),
                  _const_spec(near_tiles.shape), _const_spec(ovlt.shape), _qblock_cols(LANE, nq)],
        out_specs=[_qblock_rows(HD, nq), _qblock_cols(N_KV * n_blocks, nq)],
        out_shape=[jax.ShapeDtypeStruct((n, HD), F32), jax.ShapeDtypeStruct((N_KV * n_blocks, n), BF16)],
        scratch_shapes=[pltpu.VMEM((N_KV, nch, GROUP * Q_BLOCK), F32)],
        compiler_params=_cparams("parallel", "parallel"), name="cmp_select")(
            qt, kc, vct, near_tiles, ovlt, gate_t)


FAR_CHUNK = 512


def _selected_kernel(qt_ref, k_ref, vt_ref, selt_ref, ext_ref, bt_ref, gl_ref, o_ref, *, n_blocks):
    i = pl.program_id(1)
    gates = jax.nn.sigmoid(gl_ref[0:N_GATE, :])
    far_end = (i - 1) * Q_BLOCK
    per = FAR_CHUNK // Q_BLOCK
    n_far = (i + per - 2) // per
    qts = [_q_group_t(qt_ref, g) for g in range(N_KV)]
    gsls = [slice(g * HEAD_DIM, (g + 1) * HEAD_DIM) for g in range(N_KV)]

    def mask_bias(g, start, size):
        sel_g = selt_ref[g * n_blocks:(g + 1) * n_blocks, :]
        return (_dot(ext_ref[pl.ds(start, size), :], sel_g) - 1.0) * (-NEG)

    near = []
    for g in range(N_KV):
        s_parts, v_parts = [], []
        for j in (1, 0):
            start = pl.multiple_of(jnp.maximum(i - j, 0) * Q_BLOCK, Q_BLOCK)
            mb = mask_bias(g, start, Q_BLOCK)
            if j > 0:
                mb = mb + jnp.where(i >= j, 0.0, NEG)
            sj = _dot(k_ref[pl.ds(start, Q_BLOCK), gsls[g]], qts[g])
            s_parts.append(_heads_add(sj, [bt_ref[j, g * GROUP + r] + mb for r in range(GROUP)]))
            v_parts.append(vt_ref[gsls[g], pl.ds(start, Q_BLOCK)])
        near.append((jnp.concatenate(s_parts, axis=0), jnp.concatenate(v_parts, axis=1)))
    state0 = []
    for s, vt in near:
        m0 = jnp.max(s, axis=0, keepdims=True)
        e = jnp.exp2(s - m0)
        state0 += [m0, jnp.sum(e, axis=0, keepdims=True), _dot(vt, e.astype(BF16))]

    def far_step(c, carry):
        start = pl.multiple_of(c * FAR_CHUNK, FAR_CHUNK)
        in_far = start + lax.broadcasted_iota(jnp.int32, (FAR_CHUNK, Q_BLOCK), 0) < far_end
        scs = []
        for g in range(N_KV):
            mb = jnp.where(in_far, mask_bias(g, start, FAR_CHUNK), NEG)
            scs.append(_heads_add(_dot(k_ref[pl.ds(start, FAR_CHUNK), gsls[g]], qts[g]), [mb] * GROUP))
        new = []
        for g, sc in enumerate(scs):
            m_old, l_old, acc_old = carry[3 * g:3 * g + 3]
            m_new = jnp.maximum(m_old, jnp.max(sc, axis=0, keepdims=True))
            alpha = jnp.exp2(m_old - m_new)
            p = jnp.exp2(sc - m_new)
            new += [m_new, alpha * l_old + jnp.sum(p, axis=0, keepdims=True),
                    alpha * acc_old + _dot(vt_ref[gsls[g], pl.ds(start, FAR_CHUNK)], p.astype(BF16))]
        return tuple(new)

    final = lax.fori_loop(0, n_far, far_step, tuple(state0))
    outs = []
    for g in range(N_KV):
        _, den, acc = final[3 * g:3 * g + 3]
        outs += _head_slabs(acc * (_gate_row(gates, 1, g) / den))
    o_ref[...] = jnp.concatenate(outs, axis=0).T


def _selected(qt, kvb, vt, v_idx, selt, near_tiles, gate_t, n_batch, seq):
    assert seq % FAR_CHUNK == 0
    nq = seq // Q_BLOCK
    n_blocks = seq // SEL_BLOCK
    ext = jnp.asarray((np.arange(seq)[:, None] // SEL_BLOCK == np.arange(n_blocks)[None, :]).astype(np.float32), BF16)
    return pl.pallas_call(
        functools.partial(_selected_kernel, n_blocks=n_blocks),
        grid=(n_batch, nq),
        in_specs=[_qblock_cols(HD, nq),
                  pl.BlockSpec((seq, KVD), lambda b, i: (b, 2)),
                  pl.BlockSpec((KVD, seq), lambda b, i: (v_idx, b)),
                  _qblock_cols(N_KV * n_blocks, nq), _const_spec(ext.shape), _const_spec(near_tiles.shape),
                  _qblock_cols(LANE, nq)],
        out_specs=_qblock_rows(HD, nq),
        out_shape=jax.ShapeDtypeStruct((n_batch * seq, HD), F32),
        compiler_params=_cparams("parallel", "parallel"), name="selected_attn")(
            qt, kvb, vt, selt, ext, near_tiles, gate_t)


def _block_diag_q(q):
    z = jnp.zeros_like(q)
    row = lax.broadcasted_iota(jnp.int32, (N_HEADS, 2 * HEAD_DIM), 0)
    return jnp.where(row < GROUP, jnp.concatenate([q, z], axis=1), jnp.concatenate([z, q], axis=1))


def _pick_group_half(o_full):
    row = lax.broadcasted_iota(jnp.int32, (N_HEADS, HEAD_DIM), 0)
    return jnp.where(row < GROUP, o_full[:, :HEAD_DIM], o_full[:, HEAD_DIM:])


def _window_decode_kernel(*refs, n_samp, use_sink, use_gate):
    it = iter(refs)
    q_ref, nkv_ref, st_ref, br_ref, b0_ref = next(it), next(it), next(it), next(it), next(it)
    sink_ref = next(it) if use_sink else None
    gl_ref = next(it) if use_gate else None
    o_ref = next(it)
    for s_i in range(n_samp):
        qbd = _block_diag_q(q_ref[s_i])
        kt = st_ref[s_i, 0, 0].astype(BF16)
        vt = st_ref[s_i, 0, 1].astype(BF16)
        new = nkv_ref[s_i]
        s_buf = _dot(qbd, kt) + br_ref[...]
        s_new = jnp.sum(qbd.astype(F32) * new[:, 0:KVD], axis=1, keepdims=True) + b0_ref[...]
        m = jnp.maximum(jnp.max(s_buf, axis=1, keepdims=True), s_new)
        if use_sink:
            m = jnp.maximum(m, sink_ref[...])
        e_buf = jnp.exp(s_buf - m)
        e_new = jnp.exp(s_new - m)
        den = jnp.sum(e_buf, axis=1, keepdims=True) + e_new
        if use_sink:
            den = den + jnp.exp(sink_ref[...] - m)
        o_full = _dot_nt(e_buf.astype(BF16), vt) + e_new * new[:, KVD:2 * KVD]
        f = 1.0 / den
        if use_gate:
            f = f * jax.nn.sigmoid(gl_ref[s_i])
        o_ref[s_i] = _pick_group_half(o_full) * f


def _window_decode(q, new_kv, state, layer, bias_row, bias0, sinks, gate_logits):
    db, lb = state.shape[0], state.shape[-1]
    n_samp = 8 if db % 8 == 0 else 1
    use_sink = sinks is not None
    use_gate = gate_logits is not None
    ins = [q, new_kv.reshape(db, 1, 2 * KVD), state, bias_row, bias0]
    in_specs = [pl.BlockSpec((n_samp, N_HEADS, HEAD_DIM), lambda t: (t, 0, 0)),
                pl.BlockSpec((n_samp, 1, 2 * KVD), lambda t: (t, 0, 0)),
                pl.BlockSpec((n_samp, 1, 2, KVD, lb), lambda t: (t, layer, 0, 0, 0)),
                _const_spec(bias_row.shape), _const_spec(bias0.shape)]
    if use_sink:
        ins.append(sinks.astype(F32).reshape(N_HEADS, 1))
        in_specs.append(_const_spec((N_HEADS, 1)))
    if use_gate:
        ins.append(gate_logits)
        in_specs.append(pl.BlockSpec((n_samp, N_HEADS, 1), lambda t: (t, 0, 0)))
    return pl.pallas_call(
        functools.partial(_window_decode_kernel, n_samp=n_samp, use_sink=use_sink, use_gate=use_gate),
        grid=(db // n_samp,), in_specs=in_specs,
        out_specs=pl.BlockSpec((n_samp, N_HEADS, HEAD_DIM), lambda t: (t, 0, 0)),
        out_shape=jax.ShapeDtypeStruct((db, N_HEADS, HEAD_DIM), F32),
        compiler_params=_cparams("parallel"), name="window_decode")(*ins)


N_KIND = 4
XT_ROWS = 1024


def _nsa_decode_kernel(pt_ref, q_ref, nkv_ref, gl_ref, pool_ref, wc_ref, w1f_ref, pef_ref, w2_ref,
                       cb_ref, sb_ref, b0_ref, ovl_ref, ex_ref, oc_ref, os_ref,
                       buf_a, buf_b, xrow_k, xrow_v, sem, *, layer, n_pages, past, n_blocks, nbp):
    xrow = (xrow_k, xrow_v)
    step = pl.program_id(0)
    n_steps = pl.num_programs(0)
    nch = past // CMP_STRIDE
    n_past = past // SEL_BLOCK

    def page_copy(page, j, buf, slot):
        return pltpu.make_async_copy(
            pool_ref.at[page, layer], buf.at[:, :, pl.ds(j * PAGE_SIZE, PAGE_SIZE)], sem.at[slot])

    def start_fetch(sample, buf, slot):
        def body(j, c):
            page_copy(pt_ref[sample * n_pages + j], j, buf, slot).start()
            return c
        lax.fori_loop(0, n_pages, body, 0)

    def wait_fetch(buf, slot):
        def body(j, c):
            page_copy(0, j, buf, slot).wait()
            return c
        lax.fori_loop(0, n_pages, body, 0)

    def compute(buf, s_i):
        qbd = _block_diag_q(q_ref[s_i])
        new = nkv_ref[s_i]
        gates = jax.nn.sigmoid(gl_ref[s_i])
        s2_raw = _dot(qbd, buf[2].astype(BF16)) + sb_ref[...]
        vs_t = buf[3].astype(BF16)

        def to_rows(t):
            xr = min(XT_ROWS, past)
            for c in range(past // xr):
                xrow[t][c * xr:(c + 1) * xr, :] = buf[t, :, c * xr:(c + 1) * xr].T

        def load_xp(t, p):
            return xrow[t][pl.ds(p, nch, stride=CMP_STRIDE), :]
        kc, vc = _compress(load_xp, wc_ref, w1f_ref, pef_ref, w2_ref, nch, prepare=to_rows)
        s = _dot_nt(qbd, kc.astype(BF16)) + cb_ref[...]
        m = jnp.max(s, axis=1, keepdims=True)
        m = jnp.where(m > 0.5 * NEG, m, 0.0)
        e = jnp.exp(s - m)
        den = jnp.sum(e, axis=1, keepdims=True)
        inv = 1.0 / jnp.where(den > 0, den, 1.0)
        oc_full = _dot(e.astype(BF16), vc.astype(BF16))
        oc_ref[s_i] = _pick_group_half(oc_full) * (inv * gates[0])
        p = e * inv
        j_io = lax.broadcasted_iota(jnp.int32, (1, nbp), 1)
        cur = past // SEL_BLOCK
        forced = (j_io == 0) | (j_io == cur) | (j_io == cur - 1)
        sel_rows = []
        for g in range(N_KV):
            pg = jnp.sum(p[g * GROUP:(g + 1) * GROUP], axis=0, keepdims=True)
            imp = _dot_f32_by_01(pg, ovl_ref[...])
            score = jnp.where(j_io <= cur, jnp.where(forced, FORCED_SCORE, imp), NEG)
            sel_rows.append(_topk_mask_row(score, min(N_SELECT, n_blocks))[:, 0:n_past])
        row = lax.broadcasted_iota(jnp.int32, (N_HEADS, n_past), 0)
        sel16 = jnp.where(row < GROUP, sel_rows[0], sel_rows[1]).astype(BF16)
        selm = _dot(sel16, ex_ref[...])
        s2 = s2_raw + (selm - 1.0) * (-NEG)
        s_new = jnp.sum(qbd.astype(F32) * new[:, 0:KVD], axis=1, keepdims=True) + b0_ref[...]
        m2 = jnp.maximum(jnp.max(s2, axis=1, keepdims=True), s_new)
        e2 = jnp.exp(s2 - m2)
        e_new = jnp.exp(s_new - m2)
        den2 = jnp.sum(e2, axis=1, keepdims=True) + e_new
        os_full = _dot_nt(e2.astype(BF16), vs_t) + e_new * new[:, KVD:2 * KVD]
        os_ref[s_i] = _pick_group_half(os_full) * (gates[1] / den2)

    @pl.when(step == 0)
    def _():
        start_fetch(0, buf_a, 0)

    start_fetch(2 * step + 1, buf_b, 1)
    wait_fetch(buf_a, 0)
    compute(buf_a, 0)

    @pl.when(step + 1 < n_steps)
    def _():
        start_fetch(2 * step + 2, buf_a, 0)

    wait_fetch(buf_b, 1)
    compute(buf_b, 1)


def _nsa_decode(q, new_sel, gate_logits2, pool, page_table, layer, cw, cmp_bias, sel_bias, bias0):
    db, n_pages = page_table.shape
    assert db % 2 == 0
    past = n_pages * PAGE_SIZE
    nch = past // CMP_STRIDE
    n_past = past // SEL_BLOCK
    n_blocks = n_past + 1
    nbp = -(-n_blocks // LANE) * LANE
    wc, w1f, pef, w2bd = cw
    ovl = jnp.asarray(_overlap_np(nch, n_blocks, nbp), BF16)
    ex = jnp.asarray((np.arange(n_past)[:, None] == (np.arange(past)[None, :] // SEL_BLOCK)).astype(np.float32), BF16)
    consts = [wc, w1f, pef, w2bd, cmp_bias, sel_bias, bias0, ovl, ex]
    per2 = lambda *tail: pl.BlockSpec((2,) + tail, lambda t, pt: (t,) + (0,) * len(tail))
    grid_spec = pltpu.PrefetchScalarGridSpec(
        num_scalar_prefetch=1, grid=(db // 2,),
        in_specs=[per2(N_HEADS, HEAD_DIM), per2(1, 2 * KVD), per2(2, N_HEADS, 1),
                  pl.BlockSpec(memory_space=pl.ANY)] + [_const_spec(c.shape) for c in consts],
        out_specs=[per2(N_HEADS, HEAD_DIM), per2(N_HEADS, HEAD_DIM)],
        scratch_shapes=[pltpu.VMEM((N_KIND, KVD, past), F32), pltpu.VMEM((N_KIND, KVD, past), F32),
                        pltpu.VMEM((past, KVD), F32), pltpu.VMEM((past, KVD), F32), pltpu.SemaphoreType.DMA((2,))])
    out = jax.ShapeDtypeStruct((db, N_HEADS, HEAD_DIM), F32)
    return pl.pallas_call(
        functools.partial(_nsa_decode_kernel, layer=layer, n_pages=n_pages, past=past,
                          n_blocks=n_blocks, nbp=nbp),
        grid_spec=grid_spec, out_shape=[out, out],
        compiler_params=_cparams("arbitrary"), name="nsa_decode")(
            page_table.reshape(-1), q, new_sel.reshape(db, 1, 2 * KVD), gate_logits2, pool, *consts)


def _shift_kernel(st_ref, new_ref, o_ref, *, lb):
    x = st_ref[...]
    lane = lax.broadcasted_iota(jnp.int32, x.shape, 2)
    o_ref[...] = jnp.where(lane == lb - 1, new_ref[...], pltpu.roll(x, lb - 1, 2))


def _shift_state(state, new_rows):
    db, w, lb = state.shape
    ns = max(1, min(db, (4 << 20) // (w * lb * 4)))
    assert db % ns == 0
    blk = pl.BlockSpec((ns, w, lb), lambda t: (t, 0, 0))
    return pl.pallas_call(
        functools.partial(_shift_kernel, lb=lb), grid=(db // ns,),
        in_specs=[blk, pl.BlockSpec((ns, w, 1), lambda t: (t, 0, 0))], out_specs=blk,
        out_shape=jax.ShapeDtypeStruct((db, w, lb), state.dtype),
        compiler_params=_cparams("parallel"), name="shift_state")(state, new_rows.reshape(db, w, 1))


def _decode_bias_row(rel_bias, past, kpos, window):
    dist = past - kpos
    valid = (dist >= 0) if window is None else ((dist >= 0) & (dist < window))
    return _bias_tile(rel_bias, dist[None, :], valid[None, :], False)[:, 0, :]


def kernel(x_prompt, x_sample, state_swa, cache_nsa, state_nsa_win, page_table, rel_bias, norm_mix, norm_mlp,
           norm_final, w_in_a, b_in_a, w_out_a, b_out_a, sinks_a, w_in_b, w_out_b, w_cmp1, w_cmp2, pe_cmp,
           w_up, w_down):
    n_batch, seq, _ = x_prompt.shape
    db = x_sample.shape[0]
    assert x_sample.shape[1] == 1
    depth = norm_mix.shape[0]
    n_la, n_lb = w_in_a.shape[0], w_in_b.shape[0]
    lba, lbb = state_swa.shape[1], state_nsa_win.shape[1]
    n_pages = page_table.shape[1]
    past = n_pages * PAGE_SIZE
    assert seq >= WIN_B and lba == WIN_A and lbb == WIN_B and past >= WIN_B

    kk = np.arange(Q_BLOCK)[:, None]
    qq = np.arange(Q_BLOCK)[None, :]
    d0, d1 = qq - kk, Q_BLOCK + qq - kk
    swa_tiles = LOG2E * jnp.stack([_bias_tile(rel_bias, d0, (d0 >= 0) & (d0 < WIN_A), False),
                                   _bias_tile(rel_bias, d1, (d1 >= 0) & (d1 < WIN_A), False)])
    near_tiles = LOG2E * jnp.stack([_bias_tile(rel_bias, d0, d0 >= 0, True),
                                    _bias_tile(rel_bias, d1, d1 >= 0, True)])
    nback_b = WIN_B // Q_BLOCK
    far_mask = jnp.asarray(np.where(kk > qq, 0.0, NEG), F32)
    no_mask = jnp.zeros((8, LANE), F32)
    cc = np.arange(NEAR_CMP)[:, None]
    dc0 = qq - CMP_STRIDE * cc - (CMP_LEN - 1)
    dc1 = qq - CMP_STRIDE * (cc - NEAR_CMP // 2) - (CMP_LEN - 1)
    cmp_tiles = LOG2E * jnp.stack([_bias_tile(rel_bias, dc0, dc0 >= 0, True),
                                   _bias_tile(rel_bias, dc1, dc1 >= 0, True)])
    nch_d = past // CMP_STRIDE
    cmp_row = _decode_bias_row(rel_bias, past, np.arange(nch_d) * CMP_STRIDE + CMP_LEN - 1, None)
    sel_row = _decode_bias_row(rel_bias, past, np.arange(past), None)
    swa_row = _decode_bias_row(rel_bias, past, past - lba + np.arange(lba), WIN_A)
    win_row = _decode_bias_row(rel_bias, past, past - lbb + np.arange(lbb), WIN_B)
    bias0 = rel_bias.astype(F32)[0].reshape(N_HEADS, 1)

    xp = x_prompt.reshape(n_batch * seq, D_MODEL)
    xs = x_sample.reshape(db, D_MODEL)
    to_minor = lambda a: jnp.transpose(a, (0, 2, 3, 4, 5, 1))
    st_swa = to_minor(state_swa).reshape(db, n_la, 2, KVD, lba)
    st_win = to_minor(state_nsa_win).reshape(db, n_lb, 2, KVD, lbb)
    pool = to_minor(cache_nsa).reshape(cache_nsa.shape[0], n_lb, N_KIND, KVD, PAGE_SIZE)
    swa_p, swa_new, rows_s, win_p, win_new = [], [], [], [], []
    rows_pt = jnp.zeros((n_batch, n_lb, N_KIND * KVD, seq), F32)

    for layer in range(depth):
        final = norm_final if layer == depth - 1 else None
        if layer % 2 == 0:
            a = layer // 2
            qt_p, kv_p, kvb_p, vt_p = _project(xp, norm_mix[layer], w_in_a[a], b_in_a[a], 2 * KVD, False, (KVD,))
            q_s, kv_s, _ = _project(xs, norm_mix[layer], w_in_a[a], b_in_a[a], 2 * KVD, False, None)
            o_p = _banded(qt_p, kvb_p, 0, vt_p, 0, n_batch, seq, ("bias", "bias"), swa_tiles, no_mask,
                          sinks_a[a], None, None)
            o_s = _window_decode(jnp.swapaxes(q_s, 0, 1), kv_s, st_swa, a, swa_row, bias0, sinks_a[a], None)
            swa_p.append(kv_p.reshape(n_batch, seq, 2, N_KV, HEAD_DIM)[:, seq - WIN_A:])
            swa_new.append(kv_s)
            xp = _out_mlp(xp, [o_p], w_out_a[a], b_out_a[a], norm_mlp[layer], w_up[layer], w_down[layer], final)
            xs = _out_mlp(xs, [o_s.reshape(db, HD)], w_out_a[a], b_out_a[a], norm_mlp[layer], w_up[layer],
                          w_down[layer], final)
        else:
            b = layer // 2
            cw = _compress_weights(w_cmp1[b], w_cmp2[b], pe_cmp[b])
            w_in = jnp.pad(w_in_b[b], ((0, 0), (0, LANE - N_GATE)))
            qt_p, kv_p, kvb_p, vt_p, glt_p, rows_pt = _project(xp, norm_mix[layer], w_in, None, 6 * KVD, True,
                                                               (3 * KVD, 5 * KVD), (rows_pt, b))
            q_s, kv_s, _, gl_s = _project(xs, norm_mix[layer], w_in, None, 6 * KVD, True, None)
            kc, vct = _compress_prompt(kv_p, n_batch, seq, cw)
            o_c, selt = _cmp_select(qt_p, kc, vct, cmp_tiles, glt_p, n_batch, seq)
            o_sel = _selected(qt_p, kvb_p, vt_p, 0, selt, near_tiles, glt_p, n_batch, seq)
            kinds = ("bias", "bias") + ("none",) * (nback_b - 2) + ("mask",)
            o_w = _banded(qt_p, kvb_p, 4, vt_p, 1, n_batch, seq, kinds, near_tiles, far_mask, None, glt_p, 2)
            win_p.append(kv_p.reshape(n_batch, seq, 6 * KVD)[:, seq - WIN_B:, 4 * KVD:]
                         .reshape(n_batch, WIN_B, 2, N_KV, HEAD_DIM))
            xp = _out_mlp(xp, [o_c, o_sel, o_w], w_out_b[b], None, norm_mlp[layer], w_up[layer], w_down[layer], final)
            q_sd = jnp.swapaxes(q_s, 0, 1)
            gl_s3 = gl_s.reshape(db, 3, N_HEADS, 1)
            oc_s, os_s = _nsa_decode(q_sd, kv_s[:, 2 * KVD:4 * KVD], gl_s3[:, 0:2], pool, page_table, b, cw,
                                     cmp_row, sel_row, bias0)
            ow_s = _window_decode(q_sd, kv_s[:, 4 * KVD:], st_win, b, win_row, bias0, None, gl_s3[:, 2])
            rows_s.append(kv_s[:, :4 * KVD].reshape(db, 1, 4, N_KV, HEAD_DIM))
            win_new.append(kv_s[:, 4 * KVD:])
            xs = _out_mlp(xs, [oc_s.reshape(db, HD), os_s.reshape(db, HD), ow_s.reshape(db, HD)], w_out_b[b], None,
                          norm_mlp[layer], w_up[layer], w_down[layer], final)

    y_prompt = xp.reshape(n_batch, seq, D_MODEL)
    y_sample = xs.reshape(db, 1, D_MODEL)
    from_minor = lambda a, nl, lb: jnp.transpose(a.reshape(db, nl, 2, N_KV, HEAD_DIM, lb), (0, 5, 1, 2, 3, 4))
    swa_sample = _shift_state(st_swa.reshape(db, n_la * 2 * KVD, lba), jnp.concatenate(swa_new, axis=1))
    win_sample = _shift_state(st_win.reshape(db, n_lb * 2 * KVD, lbb), jnp.concatenate(win_new, axis=1))
    return (y_prompt, y_sample,
            jnp.stack(swa_p, axis=2),
            from_minor(swa_sample, n_la, lba),
            jnp.transpose(rows_pt.reshape(n_batch, n_lb, N_KIND, N_KV, HEAD_DIM, seq), (0, 5, 1, 2, 3, 4)),
            jnp.stack(rows_s, axis=2),
            jnp.stack(win_p, axis=2),
            from_minor(win_sample, n_lb, lbb))
```

```python
import functools
import math

import numpy as np
import jax
import jax.numpy as jnp
from jax import lax
from jax.experimental import pallas as pl
from jax.experimental.pallas import tpu as pltpu

D_MODEL = 1024
N_HEADS = 16
HEAD_DIM = 64
N_KV = 2
GROUP = N_HEADS // N_KV
HD = N_HEADS * HEAD_DIM
KVD = N_KV * HEAD_DIM
D_FF = 4 * D_MODEL
PAGE_SIZE = 128
WIN_A = 128
WIN_B = 512
Q_BLOCK = 128
CMP_STRIDE = 16
CMP_LEN = 2 * CMP_STRIDE
CMP_HIDDEN = 128
SEL_BLOCK = 64
N_SELECT = 16
FORCED_SCORE = 1e9
N_BUCKETS = 32
MAX_DISTANCE = 128
NORM_EPS = 1e-5
SCALE = HEAD_DIM ** -0.5
LOG2E = math.log2(math.e)
N_GATE = 3 * N_HEADS
LANE = 128

NEG = -1e30
REMOVED = -2e30
F32 = jnp.float32
BF16 = jnp.bfloat16
VMEM_LIMIT = 56 * 1024 * 1024


def _cparams(*sem):
    return pltpu.CompilerParams(dimension_semantics=sem, vmem_limit_bytes=VMEM_LIMIT)


def _const_spec(shape):
    nd = len(shape)
    return pl.BlockSpec(shape, lambda *_: (0,) * nd, pipeline_mode=pl.Buffered(1))


def _bucket_np(dist):
    n = np.maximum(dist, 0)
    exact = N_BUCKETS // 2
    nf = np.maximum(n, exact).astype(np.float32)
    large = exact + (np.log(nf / np.float32(exact)) / np.float32(math.log(MAX_DISTANCE / exact))
                     * np.float32(N_BUCKETS - exact)).astype(np.int32)
    return np.where(n < exact, n, np.minimum(large, N_BUCKETS - 1)).astype(np.int32)


def _bias_tile(rel_bias, dist, valid, shift):
    tb = rel_bias.astype(F32)
    if shift:
        tb = tb - tb[N_BUCKETS - 1][None, :]
    bucket = jnp.asarray(_bucket_np(dist).reshape(1, -1))
    onehot = (bucket == jnp.arange(N_BUCKETS, dtype=jnp.int32)[:, None]).astype(F32)
    b = jnp.dot(tb.T, onehot, precision=lax.Precision.HIGHEST).reshape((N_HEADS,) + dist.shape)
    return jnp.where(jnp.asarray(valid)[None], b, NEG)


def _rms(x, g):
    ms = jnp.mean(x * x, axis=-1, keepdims=True)
    return x * lax.rsqrt(ms + NORM_EPS) * g


def _dot(a, b):
    return jnp.dot(a, b, preferred_element_type=F32)


def _dot_nt(a, b):
    return lax.dot_general(a, b, (((1,), (1,)), ((), ())), preferred_element_type=F32)


def _split3(a):
    hi = a.astype(BF16)
    r1 = a - hi.astype(F32)
    mid = r1.astype(BF16)
    lo = (r1 - mid.astype(F32)).astype(BF16)
    return hi, mid, lo


def _dot_f32_by_01(a, b01):
    hi, mid, lo = _split3(a)
    return _dot(hi, b01) + _dot(mid, b01) + _dot(lo, b01)


def _dot_01_by_f32(b01, a):
    hi, mid, lo = _split3(a)
    return _dot(b01, hi) + _dot(b01, mid) + _dot(b01, lo)


def _topk_mask(score, k, axis):
    n = score.shape[axis]
    iota = lax.broadcasted_iota(jnp.int32, score.shape, axis).astype(F32)
    sel = jnp.zeros_like(score)
    s = score
    for _ in range(k):
        m = jnp.max(s, axis=axis, keepdims=True)
        first = jnp.min(jnp.where(s == m, iota, float(n)), axis=axis, keepdims=True)
        hit = iota == first
        ok = jnp.where(m > 0.5 * NEG, 1.0, 0.0)
        sel = jnp.where(hit, ok, sel)
        s = jnp.where(hit, REMOVED, s)
    return sel


def _topk_mask_row(score, k):
    n = score.shape[1]
    s_j = jnp.broadcast_to(score, (n, n))
    s_i = jnp.concatenate([jnp.broadcast_to(score, (LANE, n)).T] * (n // LANE), axis=1)
    i_io = lax.broadcasted_iota(jnp.int32, (n, n), 0)
    j_io = lax.broadcasted_iota(jnp.int32, (n, n), 1)
    ahead = jnp.where(s_i > s_j, 1.0, jnp.where(s_i == s_j, jnp.where(i_io < j_io, 1.0, 0.0), 0.0))
    rank = jnp.sum(ahead, axis=0, keepdims=True)
    return jnp.where(rank < k, jnp.where(score > 0.5 * NEG, 1.0, 0.0), 0.0)


def _q_group_t(qt_ref, g):
    return jnp.concatenate(
        [qt_ref[(g * GROUP + r) * HEAD_DIM:(g * GROUP + r + 1) * HEAD_DIM, :] for r in range(GROUP)], axis=1)


def _heads_add(s, tiles):
    return jnp.concatenate([s[:, r * Q_BLOCK:(r + 1) * Q_BLOCK] + tiles[r] for r in range(GROUP)], axis=1)


def _head_row(rows):
    return jnp.concatenate(rows, axis=1)


def _head_slabs(acc_t):
    return [acc_t[:, r * Q_BLOCK:(r + 1) * Q_BLOCK] for r in range(GROUP)]


def _proj_kernel(*refs, n_kv, has_bias, has_gate, v_cols, n_rows):
    it = iter(refs)
    x_ref, g_ref, w_ref = next(it), next(it), next(it)
    b_ref = next(it) if has_bias else None
    if n_rows:
        next(it)
    q_ref, kv_ref, kvb_ref = next(it), next(it), next(it)
    vt_ref = next(it) if v_cols is not None else None
    gl_ref = next(it) if has_gate else None
    rows_ref = next(it) if n_rows else None
    xn = _rms(x_ref[...], g_ref[...]).astype(BF16)
    u = _dot(xn, w_ref[...])
    if has_bias:
        u = u + b_ref[...]
    kv = u[:, HD:HD + n_kv]
    kv_ref[...] = kv
    kvb_ref[...] = kv.astype(BF16)
    if v_cols is None:
        for h in range(N_HEADS):
            q_ref[h] = (u[:, h * HEAD_DIM:(h + 1) * HEAD_DIM] * SCALE).astype(BF16)
        if has_gate:
            gl_ref[...] = u[:, HD + n_kv:HD + n_kv + N_GATE]
    else:
        q_ref[...] = (u[:, :HD] * (SCALE * LOG2E)).T.astype(BF16)
        for idx, c0 in enumerate(v_cols):
            vt_ref[idx * KVD:(idx + 1) * KVD, :] = u[:, HD + c0:HD + c0 + KVD].T.astype(BF16)
        if has_gate:
            gl_ref[...] = u[:, HD + n_kv:HD + n_kv + LANE].T
        if n_rows:
            rows_ref[...] = u[:, HD:HD + n_rows].T


def _project(x, g, w, b, n_kv, has_gate, v_cols, rows_into=None):
    n = x.shape[0]
    dout = w.shape[1]
    tm = min(512, n)
    assert n % tm == 0
    has_bias = b is not None
    prompt = v_cols is not None
    ins = [x, g.reshape(1, D_MODEL), w.astype(BF16)]
    in_specs = [pl.BlockSpec((tm, D_MODEL), lambda t: (t, 0)),
                _const_spec((1, D_MODEL)), _const_spec((D_MODEL, dout))]
    if has_bias:
        ins.append(b.reshape(1, dout))
        in_specs.append(_const_spec((1, dout)))
    aliases = {}
    n_rows = 0
    if rows_into is not None:
        rows_arr, rows_layer = rows_into
        n_rows, seq = rows_arr.shape[2], rows_arr.shape[3]
        assert seq % tm == 0
        aliases = {len(ins): 0}
        ins.append(rows_arr)
        in_specs.append(pl.BlockSpec(memory_space=pl.ANY))
    rows = lambda wd, dt: (jax.ShapeDtypeStruct((n, wd), dt), pl.BlockSpec((tm, wd), lambda t: (t, 0)))
    cols = lambda ht, dt: (jax.ShapeDtypeStruct((ht, n), dt), pl.BlockSpec((ht, tm), lambda t: (0, t)))
    if prompt:
        outs = [cols(HD, BF16), rows(n_kv, F32), rows(n_kv, BF16), cols(len(v_cols) * KVD, BF16)]
        if has_gate:
            outs.append(cols(LANE, F32))
    else:
        outs = [(jax.ShapeDtypeStruct((N_HEADS, n, HEAD_DIM), BF16),
                 pl.BlockSpec((N_HEADS, tm, HEAD_DIM), lambda t: (0, t, 0))),
                rows(n_kv, F32), rows(n_kv, BF16)]
        if has_gate:
            outs.append(rows(N_GATE, F32))
    if n_rows:
        spb = seq // tm
        aliases = {k: len(outs) for k in aliases}
        outs.append((jax.ShapeDtypeStruct(rows_arr.shape, rows_arr.dtype),
                     pl.BlockSpec((None, None, n_rows, tm), lambda t: (t // spb, rows_layer, 0, t % spb))))
    return pl.pallas_call(
        functools.partial(_proj_kernel, n_kv=n_kv, has_bias=has_bias, has_gate=has_gate, v_cols=v_cols,
                          n_rows=n_rows),
        grid=(n // tm,), in_specs=in_specs, out_specs=[o[1] for o in outs], out_shape=[o[0] for o in outs],
        input_output_aliases=aliases,
        compiler_params=_cparams("parallel"), name="norm_proj")(*ins)


FF_CHUNK = 1024


def _out_mlp_kernel(*refs, n_o, has_bias, final):
    it = iter(refs)
    x_ref = next(it)
    o_refs = [next(it) for _ in range(n_o)]
    wo_ref = next(it)
    bo_ref = next(it) if has_bias else None
    gm_ref, wup_ref, wdn_ref = next(it), next(it), next(it)
    gf_ref = next(it) if final else None
    y_ref = next(it)
    o = o_refs[0][...]
    for r in o_refs[1:]:
        o = o + r[...]
    x1 = x_ref[...] + _dot(o.astype(BF16), wo_ref[...])
    if has_bias:
        x1 = x1 + bo_ref[...]
    xn = _rms(x1, gm_ref[...]).astype(BF16)
    acc = x1
    for c in range(D_FF // FF_CHUNK):
        h = jnp.maximum(_dot(xn, wup_ref[:, c * FF_CHUNK:(c + 1) * FF_CHUNK]), 0.0)
        acc = acc + _dot((h * h).astype(BF16), wdn_ref[c * FF_CHUNK:(c + 1) * FF_CHUNK, :])
    y_ref[...] = _rms(acc, gf_ref[...]) if final else acc


def _out_mlp(x, os_, w_out, b_out, g_mlp, w_up, w_down, g_final):
    n = x.shape[0]
    tm = min(512, n)
    assert n % tm == 0
    has_bias = b_out is not None
    final = g_final is not None
    row = pl.BlockSpec((tm, D_MODEL), lambda t: (t, 0))
    ins = [x, *os_, w_out.astype(BF16)]
    in_specs = [row] + [row] * len(os_) + [_const_spec((HD, D_MODEL))]
    if has_bias:
        ins.append(b_out.reshape(1, D_MODEL))
        in_specs.append(_const_spec((1, D_MODEL)))
    ins += [g_mlp.reshape(1, D_MODEL), w_up.astype(BF16), w_down.astype(BF16)]
    in_specs += [_const_spec((1, D_MODEL)), _const_spec((D_MODEL, D_FF)), _const_spec((D_FF, D_MODEL))]
    if final:
        ins.append(g_final.reshape(1, D_MODEL))
        in_specs.append(_const_spec((1, D_MODEL)))
    return pl.pallas_call(
        functools.partial(_out_mlp_kernel, n_o=len(os_), has_bias=has_bias, final=final),
        grid=(n // tm,), in_specs=in_specs, out_specs=row,
        out_shape=jax.ShapeDtypeStruct((n, D_MODEL), F32),
        compiler_params=_cparams("parallel"), name="out_mlp")(*ins)


def _qblock_cols(height, nq):
    return pl.BlockSpec((height, Q_BLOCK), lambda b, i: (0, b * nq + i))


def _qblock_rows(width, nq):
    return pl.BlockSpec((Q_BLOCK, width), lambda b, i: (b * nq + i, 0))


def _gate_row(gates, branch, g):
    return _head_row([gates[branch * N_HEADS + g * GROUP + r:branch * N_HEADS + g * GROUP + r + 1, :]
                      for r in range(GROUP)])


def _banded_kernel(*refs, kinds, use_sink, gate_branch):
    it = iter(refs)
    qt_ref, k_ref, vt_ref, bt_ref, mt_ref = next(it), next(it), next(it), next(it), next(it)
    sink_ref = next(it) if use_sink else None
    gl_ref = next(it) if gate_branch is not None else None
    o_ref = next(it)
    i = pl.program_id(1)
    gates = jax.nn.sigmoid(gl_ref[0:N_GATE, :]) if gate_branch is not None else None
    bias_slot = {}
    for j, kind in enumerate(kinds):
        if kind == "bias":
            bias_slot[j] = len(bias_slot)
    scores = []
    for g in range(N_KV):
        qt = _q_group_t(qt_ref, g)
        gsl = slice(g * HEAD_DIM, (g + 1) * HEAD_DIM)
        s_parts, v_parts = [], []
        for j, kind in enumerate(kinds):
            start = pl.multiple_of(jnp.maximum(i - j, 0) * Q_BLOCK, Q_BLOCK)
            v_parts.append(vt_ref[gsl, pl.ds(start, Q_BLOCK)])
            sj = _dot(k_ref[pl.ds(start, Q_BLOCK), gsl], qt)
            pen = jnp.where(i >= j, 0.0, NEG) if j > 0 else 0.0
            if kind == "bias":
                sj = _heads_add(sj, [bt_ref[bias_slot[j], g * GROUP + r] + pen for r in range(GROUP)])
            elif kind == "mask":
                sj = _heads_add(sj, [mt_ref[...] + pen] * GROUP)
            else:
                sj = sj + pen
            s_parts.append(sj)
        scores.append((jnp.concatenate(s_parts, axis=0), jnp.concatenate(v_parts, axis=1)))
    outs = []
    for g, (s, vt) in enumerate(scores):
        m = jnp.max(s, axis=0, keepdims=True)
        if use_sink:
            sk = _head_row([jnp.full((1, Q_BLOCK), sink_ref[g * GROUP + r] * LOG2E, F32) for r in range(GROUP)])
            m = jnp.maximum(m, sk)
        e = jnp.exp2(s - m)
        den = jnp.sum(e, axis=0, keepdims=True)
        if use_sink:
            den = den + jnp.exp2(sk - m)
        f = 1.0 / den
        if gates is not None:
            f = f * _gate_row(gates, gate_branch, g)
        outs += _head_slabs(_dot(vt, e.astype(BF16)) * f)
    o_ref[...] = jnp.concatenate(outs, axis=0).T


def _banded(qt, kvb, k_col, vt, v_idx, n_batch, seq, kinds, bias_tiles, mask_tile, sinks, gate_t, gate_branch):
    nq = seq // Q_BLOCK
    use_sink = sinks is not None
    ins = [qt, kvb, vt, bias_tiles, mask_tile]
    in_specs = [_qblock_cols(HD, nq),
                pl.BlockSpec((seq, KVD), lambda b, i: (b, k_col)),
                pl.BlockSpec((KVD, seq), lambda b, i: (v_idx, b)),
                _const_spec(bias_tiles.shape), _const_spec(mask_tile.shape)]
    if use_sink:
        ins.append(sinks.astype(F32))
        in_specs.append(pl.BlockSpec(memory_space=pltpu.SMEM))
    if gate_t is not None:
        ins.append(gate_t)
        in_specs.append(_qblock_cols(LANE, nq))
    else:
        gate_branch = None
    return pl.pallas_call(
        functools.partial(_banded_kernel, kinds=kinds, use_sink=use_sink, gate_branch=gate_branch),
        grid=(n_batch, nq), in_specs=in_specs, out_specs=_qblock_rows(HD, nq),
        out_shape=jax.ShapeDtypeStruct((n_batch * seq, HD), F32),
        compiler_params=_cparams("parallel", "parallel"), name="banded_attn")(*ins)


def _compress(load_xp, wc_ref, w1f_ref, pef_ref, w2_ref, nch, prepare=None):
    outs = []
    if prepare is not None:
        for t in range(2):
            prepare(t)
    for t in range(2):
        xs = jnp.concatenate([load_xp(t, p).astype(BF16) for p in range(CMP_STRIDE)], axis=1)
        acc = _dot(xs, wc_ref[t])
        pet = _dot(pef_ref[t], w1f_ref[t])
        lo = acc[:, :2 * CMP_HIDDEN]
        hi_next = pltpu.roll(acc[:, 2 * CMP_HIDDEN:], nch - 1, 0)
        hh = lo + hi_next + jnp.concatenate([pet, pet], axis=1)
        a = (hh * jax.nn.sigmoid(hh)).astype(BF16)
        outs.append(_dot(a, w2_ref[t]))
    return outs


def _compress_weights(w1, w2, pe):
    z = jnp.zeros((2, CMP_STRIDE, HEAD_DIM, CMP_HIDDEN), w1.dtype)
    lo, hi = w1[:, :CMP_STRIDE], w1[:, CMP_STRIDE:]
    top = jnp.concatenate([lo, z, hi, z], axis=-1)
    bot = jnp.concatenate([z, lo, z, hi], axis=-1)
    wp = jnp.concatenate([top, bot], axis=2)
    wc = wp.reshape(2, CMP_STRIDE * KVD, 4 * CMP_HIDDEN).astype(BF16)
    z2 = jnp.zeros((2, CMP_HIDDEN, HEAD_DIM), w2.dtype)
    w2bd = jnp.concatenate([jnp.concatenate([w2, z2], axis=-1), jnp.concatenate([z2, w2], axis=-1)], axis=1)
    w1f = w1.reshape(2, CMP_LEN * HEAD_DIM, CMP_HIDDEN).astype(BF16)
    pef = pe.reshape(2, 1, CMP_LEN * HEAD_DIM).astype(BF16)
    return wc, w1f, pef, w2bd.astype(BF16)


def _compress_prompt_kernel(x_ref, wc_ref, w1f_ref, pef_ref, w2_ref, kc_ref, vct_ref, *, nch):
    def load_xp(t, p):
        return x_ref[:, p, t * KVD:(t + 1) * KVD]
    kc, vc = _compress(load_xp, wc_ref, w1f_ref, pef_ref, w2_ref, nch)
    kc_ref[...] = kc
    vct_ref[...] = vc.T.astype(BF16)


def _compress_prompt(kvf, n_batch, seq, cw):
    nch = seq // CMP_STRIDE
    wc, w1f, pef, w2bd = cw
    return pl.pallas_call(
        functools.partial(_compress_prompt_kernel, nch=nch),
        grid=(n_batch,),
        in_specs=[pl.BlockSpec((nch, CMP_STRIDE, 2 * KVD), lambda b: (b, 0, 0)),
                  _const_spec(wc.shape), _const_spec(w1f.shape), _const_spec(pef.shape), _const_spec(w2bd.shape)],
        out_specs=[pl.BlockSpec((nch, KVD), lambda b: (b, 0)), pl.BlockSpec((None, KVD, nch), lambda b: (b, 0, 0))],
        out_shape=[jax.ShapeDtypeStruct((n_batch * nch, KVD), F32), jax.ShapeDtypeStruct((n_batch, KVD, nch), BF16)],
        compiler_params=_cparams("parallel"), name="compress_prompt")(
            kvf.reshape(n_batch * nch, CMP_STRIDE, kvf.shape[1]), wc, w1f, pef, w2bd)


NEAR_CMP = 16


def _cmp_select_kernel(qt_ref, kc_ref, vct_ref, nt_ref, ovlt_ref, gl_ref, o_ref, selt_ref, s_scr, *, nch, n_blocks):
    i = pl.program_id(1)
    gates = jax.nn.sigmoid(gl_ref[0:N_GATE, :])
    cs = i * (Q_BLOCK // CMP_STRIDE) - NEAR_CMP // 2
    start = pl.multiple_of(jnp.maximum(cs, 0), 8)
    variant = jnp.minimum(i, 1)
    far_ok = lax.broadcasted_iota(jnp.int32, (nch, Q_BLOCK), 0) < cs
    kc_all = kc_ref[...].astype(BF16)
    kc_near = kc_ref[pl.ds(start, NEAR_CMP), :].astype(BF16)
    j_io = lax.broadcasted_iota(jnp.int32, (n_blocks, Q_BLOCK), 0)
    q_io = lax.broadcasted_iota(jnp.int32, (n_blocks, Q_BLOCK), 1)
    cur = (i * Q_BLOCK + q_io) // SEL_BLOCK
    forced = (j_io == 0) | (j_io == cur) | (j_io == cur - 1)
    for g in range(N_KV):
        qt = _q_group_t(qt_ref, g)
        gsl = slice(g * HEAD_DIM, (g + 1) * HEAD_DIM)
        s_far = _dot(kc_all[:, gsl], qt)
        s_scr[g] = jnp.concatenate([jnp.where(far_ok, sl, NEG) for sl in _head_slabs(s_far)], axis=1)
        s_scr[g, pl.ds(start, NEAR_CMP), :] = _heads_add(
            _dot(kc_near[:, gsl], qt), [nt_ref[variant, g * GROUP + r] for r in range(GROUP)])
    outs, sels = [], []
    for g in range(N_KV):
        gsl = slice(g * HEAD_DIM, (g + 1) * HEAD_DIM)
        s = s_scr[g]
        m = jnp.max(s, axis=0, keepdims=True)
        m = jnp.where(m > 0.5 * NEG, m, 0.0)
        e = jnp.exp2(s - m)
        den = jnp.sum(e, axis=0, keepdims=True)
        inv = 1.0 / jnp.where(den > 0, den, 1.0)
        outs += _head_slabs(_dot(vct_ref[gsl, :], e.astype(BF16)) * (inv * _gate_row(gates, 0, g)))
        p = e * inv
        pg = None
        for sl in _head_slabs(p):
            pg = sl if pg is None else pg + sl
        imp = _dot_01_by_f32(ovlt_ref[...], pg)
        score = jnp.where(j_io <= cur, jnp.where(forced, FORCED_SCORE, imp), NEG)
        sels.append(_topk_mask(score, min(N_SELECT, n_blocks), 0))
    o_ref[...] = jnp.concatenate(outs, axis=0).T
    selt_ref[...] = jnp.concatenate(sels, axis=0).astype(BF16)


def _overlap_np(nch, n_blocks, n_cols):
    ci = np.arange(nch)[:, None] * CMP_STRIDE
    sj = np.arange(n_cols)[None, :] * SEL_BLOCK
    ov = (ci < sj + SEL_BLOCK) & (ci + CMP_LEN > sj) & (np.arange(n_cols)[None, :] < n_blocks)
    ov[nch - 1] = False
    return ov.astype(np.float32)


def _cmp_select(qt, kc, vct, near_tiles, gate_t, n_batch, seq):
    nq = seq // Q_BLOCK
    nch = seq // CMP_STRIDE
    n_blocks = seq // SEL_BLOCK
    n = n_batch * seq
    ovlt = jnp.asarray(_overlap_np(nch, n_blocks, n_blocks).T, BF16)
    return pl.pallas_call(
        functools.partial(_cmp_select_kernel, nch=nch, n_blocks=n_blocks),
        grid=(n_batch, nq),
        in_specs=[_qblock_cols(HD, nq),
                  pl.BlockSpec((nch, KVD), lambda b, i: (b, 0)),
                  pl.BlockSpec((None, KVD, nch), lambda b, i: (b, 0, 0)),
                  _const_spec(near_tiles.shape), _const_spec(ovlt.shape), _qblock_cols(LANE, nq)],
        out_specs=[_qblock_rows(HD, nq), _qblock_cols(N_KV * n_blocks, nq)],
        out_shape=[jax.ShapeDtypeStruct((n, HD), F32), jax.ShapeDtypeStruct((N_KV * n_blocks, n), BF16)],
        scratch_shapes=[pltpu.VMEM((N_KV, nch, GROUP * Q_BLOCK), F32)],
        compiler_params=_cparams("parallel", "parallel"), name="cmp_select")(
            qt, kc, vct, near_tiles, ovlt, gate_t)


FAR_CHUNK = 512


def _selected_kernel(qt_ref, k_ref, vt_ref, selt_ref, ext_ref, bt_ref, gl_ref, o_ref, *, n_blocks):
    i = pl.program_id(1)
    gates = jax.nn.sigmoid(gl_ref[0:N_GATE, :])
    far_end = (i - 1) * Q_BLOCK
    per = FAR_CHUNK // Q_BLOCK
    n_far = (i + per - 2) // per
    qts = [_q_group_t(qt_ref, g) for g in range(N_KV)]
    gsls = [slice(g * HEAD_DIM, (g + 1) * HEAD_DIM) for g in range(N_KV)]

    def mask_bias(g, start, size):
        sel_g = selt_ref[g * n_blocks:(g + 1) * n_blocks, :]
        return (_dot(ext_ref[pl.ds(start, size), :], sel_g) - 1.0) * (-NEG)

    near = []
    for g in range(N_KV):
        s_parts, v_parts = [], []
        for j in (1, 0):
            start = pl.multiple_of(jnp.maximum(i - j, 0) * Q_BLOCK, Q_BLOCK)
            mb = mask_bias(g, start, Q_BLOCK)
            if j > 0:
                mb = mb + jnp.where(i >= j, 0.0, NEG)
            sj = _dot(k_ref[pl.ds(start, Q_BLOCK), gsls[g]], qts[g])
            s_parts.append(_heads_add(sj, [bt_ref[j, g * GROUP + r] + mb for r in range(GROUP)]))
            v_parts.append(vt_ref[gsls[g], pl.ds(start, Q_BLOCK)])
        near.append((jnp.concatenate(s_parts, axis=0), jnp.concatenate(v_parts, axis=1)))
    state0 = []
    for s, vt in near:
        m0 = jnp.max(s, axis=0, keepdims=True)
        e = jnp.exp2(s - m0)
        state0 += [m0, jnp.sum(e, axis=0, keepdims=True), _dot(vt, e.astype(BF16))]

    def far_step(c, carry):
        start = pl.multiple_of(c * FAR_CHUNK, FAR_CHUNK)
        in_far = start + lax.broadcasted_iota(jnp.int32, (FAR_CHUNK, Q_BLOCK), 0) < far_end
        scs = []
        for g in range(N_KV):
            mb = jnp.where(in_far, mask_bias(g, start, FAR_CHUNK), NEG)
            scs.append(_heads_add(_dot(k_ref[pl.ds(start, FAR_CHUNK), gsls[g]], qts[g]), [mb] * GROUP))
        new = []
        for g, sc in enumerate(scs):
            m_old, l_old, acc_old = carry[3 * g:3 * g + 3]
            m_new = jnp.maximum(m_old, jnp.max(sc, axis=0, keepdims=True))
            alpha = jnp.exp2(m_old - m_new)
            p = jnp.exp2(sc - m_new)
            new += [m_new, alpha * l_old + jnp.sum(p, axis=0, keepdims=True),
                    alpha * acc_old + _dot(vt_ref[gsls[g], pl.ds(start, FAR_CHUNK)], p.astype(BF16))]
        return tuple(new)

    final = lax.fori_loop(0, n_far, far_step, tuple(state0))
    outs = []
    for g in range(N_KV):
        _, den, acc = final[3 * g:3 * g + 3]
        outs += _head_slabs(acc * (_gate_row(gates, 1, g) / den))
    o_ref[...] = jnp.concatenate(outs, axis=0).T


def _selected(qt, kvb, vt, v_idx, selt, near_tiles, gate_t, n_batch, seq):
    assert seq % FAR_CHUNK == 0
    nq = seq // Q_BLOCK
    n_blocks = seq // SEL_BLOCK
    ext = jnp.asarray((np.arange(seq)[:, None] // SEL_BLOCK == np.arange(n_blocks)[None, :]).astype(np.float32), BF16)
    return pl.pallas_call(
        functools.partial(_selected_kernel, n_blocks=n_blocks),
        grid=(n_batch, nq),
        in_specs=[_qblock_cols(HD, nq),
                  pl.BlockSpec((seq, KVD), lambda b, i: (b, 2)),
                  pl.BlockSpec((KVD, seq), lambda b, i: (v_idx, b)),
                  _qblock_cols(N_KV * n_blocks, nq), _const_spec(ext.shape), _const_spec(near_tiles.shape),
                  _qblock_cols(LANE, nq)],
        out_specs=_qblock_rows(HD, nq),
        out_shape=jax.ShapeDtypeStruct((n_batch * seq, HD), F32),
        compiler_params=_cparams("parallel", "parallel"), name="selected_attn")(
            qt, kvb, vt, selt, ext, near_tiles, gate_t)


def _block_diag_q(q):
    z = jnp.zeros_like(q)
    row = lax.broadcasted_iota(jnp.int32, (N_HEADS, 2 * HEAD_DIM), 0)
    return jnp.where(row < GROUP, jnp.concatenate([q, z], axis=1), jnp.concatenate([z, q], axis=1))


def _pick_group_half(o_full):
    row = lax.broadcasted_iota(jnp.int32, (N_HEADS, HEAD_DIM), 0)
    return jnp.where(row < GROUP, o_full[:, :HEAD_DIM], o_full[:, HEAD_DIM:])


def _window_decode_kernel(*refs, n_samp, use_sink, use_gate):
    it = iter(refs)
    q_ref, nkv_ref, st_ref, br_ref, b0_ref = next(it), next(it), next(it), next(it), next(it)
    sink_ref = next(it) if use_sink else None
    gl_ref = next(it) if use_gate else None
    o_ref = next(it)
    for s_i in range(n_samp):
        qbd = _block_diag_q(q_ref[s_i])
        kt = st_ref[s_i, 0, 0].astype(BF16)
        vt = st_ref[s_i, 0, 1].astype(BF16)
        new = nkv_ref[s_i]
        s_buf = _dot(qbd, kt) + br_ref[...]
        s_new = jnp.sum(qbd.astype(F32) * new[:, 0:KVD], axis=1, keepdims=True) + b0_ref[...]
        m = jnp.maximum(jnp.max(s_buf, axis=1, keepdims=True), s_new)
        if use_sink:
            m = jnp.maximum(m, sink_ref[...])
        e_buf = jnp.exp(s_buf - m)
        e_new = jnp.exp(s_new - m)
        den = jnp.sum(e_buf, axis=1, keepdims=True) + e_new
        if use_sink:
            den = den + jnp.exp(sink_ref[...] - m)
        o_full = _dot_nt(e_buf.astype(BF16), vt) + e_new * new[:, KVD:2 * KVD]
        f = 1.0 / den
        if use_gate:
            f = f * jax.nn.sigmoid(gl_ref[s_i])
        o_ref[s_i] = _pick_group_half(o_full) * f


def _window_decode(q, new_kv, state, layer, bias_row, bias0, sinks, gate_logits):
    db, lb = state.shape[0], state.shape[-1]
    n_samp = 8 if db % 8 == 0 else 1
    use_sink = sinks is not None
    use_gate = gate_logits is not None
    ins = [q, new_kv.reshape(db, 1, 2 * KVD), state, bias_row, bias0]
    in_specs = [pl.BlockSpec((n_samp, N_HEADS, HEAD_DIM), lambda t: (t, 0, 0)),
                pl.BlockSpec((n_samp, 1, 2 * KVD), lambda t: (t, 0, 0)),
                pl.BlockSpec((n_samp, 1, 2, KVD, lb), lambda t: (t, layer, 0, 0, 0)),
                _const_spec(bias_row.shape), _const_spec(bias0.shape)]
    if use_sink:
        ins.append(sinks.astype(F32).reshape(N_HEADS, 1))
        in_specs.append(_const_spec((N_HEADS, 1)))
    if use_gate:
        ins.append(gate_logits)
        in_specs.append(pl.BlockSpec((n_samp, N_HEADS, 1), lambda t: (t, 0, 0)))
    return pl.pallas_call(
        functools.partial(_window_decode_kernel, n_samp=n_samp, use_sink=use_sink, use_gate=use_gate),
        grid=(db // n_samp,), in_specs=in_specs,
        out_specs=pl.BlockSpec((n_samp, N_HEADS, HEAD_DIM), lambda t: (t, 0, 0)),
        out_shape=jax.ShapeDtypeStruct((db, N_HEADS, HEAD_DIM), F32),
        compiler_params=_cparams("parallel"), name="window_decode")(*ins)


N_KIND = 4
XT_ROWS = 1024


def _nsa_decode_kernel(pt_ref, q_ref, nkv_ref, gl_ref, pool_ref, wc_ref, w1f_ref, pef_ref, w2_ref,
                       cb_ref, sb_ref, b0_ref, ovl_ref, ex_ref, oc_ref, os_ref,
                       buf_a, buf_b, xrow_k, xrow_v, sem, *, layer, n_pages, past, n_blocks, nbp):
    xrow = (xrow_k, xrow_v)
    step = pl.program_id(0)
    n_steps = pl.num_programs(0)
    nch = past // CMP_STRIDE
    n_past = past // SEL_BLOCK

    def page_copy(page, j, buf, slot):
        return pltpu.make_async_copy(
            pool_ref.at[page, layer], buf.at[:, :, pl.ds(j * PAGE_SIZE, PAGE_SIZE)], sem.at[slot])

    def start_fetch(sample, buf, slot):
        def body(j, c):
            page_copy(pt_ref[sample * n_pages + j], j, buf, slot).start()
            return c
        lax.fori_loop(0, n_pages, body, 0)

    def wait_fetch(buf, slot):
        def body(j, c):
            page_copy(0, j, buf, slot).wait()
            return c
        lax.fori_loop(0, n_pages, body, 0)

    def compute(buf, s_i):
        qbd = _block_diag_q(q_ref[s_i])
        new = nkv_ref[s_i]
        gates = jax.nn.sigmoid(gl_ref[s_i])
        s2_raw = _dot(qbd, buf[2].astype(BF16)) + sb_ref[...]

        def to_rows(t):
            xr = min(XT_ROWS, past)
            for c in range(past // xr):
                xrow[t][c * xr:(c + 1) * xr, :] = buf[t, :, c * xr:(c + 1) * xr].T

        def load_xp(t, p):
            return xrow[t][pl.ds(p, nch, stride=CMP_STRIDE), :]
        kc, vc = _compress(load_xp, wc_ref, w1f_ref, pef_ref, w2_ref, nch, prepare=to_rows)
        s = _dot_nt(qbd, kc.astype(BF16)) + cb_ref[...]
        m = jnp.max(s, axis=1, keepdims=True)
        m = jnp.where(m > 0.5 * NEG, m, 0.0)
        e = jnp.exp(s - m)
        den = jnp.sum(e, axis=1, keepdims=True)
        inv = 1.0 / jnp.where(den > 0, den, 1.0)
        oc_full = _dot(e.astype(BF16), vc.astype(BF16))
        oc_ref[s_i] = _pick_group_half(oc_full) * (inv * gates[0])
        p = e * inv
        j_io = lax.broadcasted_iota(jnp.int32, (1, nbp), 1)
        cur = past // SEL_BLOCK
        forced = (j_io == 0) | (j_io == cur) | (j_io == cur - 1)
        sel_rows = []
        for g in range(N_KV):
            pg = jnp.sum(p[g * GROUP:(g + 1) * GROUP], axis=0, keepdims=True)
            imp = _dot_f32_by_01(pg, ovl_ref[...])
            score = jnp.where(j_io <= cur, jnp.where(forced, FORCED_SCORE, imp), NEG)
            sel_rows.append(_topk_mask_row(score, min(N_SELECT, n_blocks))[:, 0:n_past])
        row = lax.broadcasted_iota(jnp.int32, (N_HEADS, n_past), 0)
        sel16 = jnp.where(row < GROUP, sel_rows[0], sel_rows[1]).astype(BF16)
        selm = _dot(sel16, ex_ref[...])
        s2 = s2_raw + (selm - 1.0) * (-NEG)
        s_new = jnp.sum(qbd.astype(F32) * new[:, 0:KVD], axis=1, keepdims=True) + b0_ref[...]
        m2 = jnp.maximum(jnp.max(s2, axis=1, keepdims=True), s_new)
        e2 = jnp.exp(s2 - m2)
        e_new = jnp.exp(s_new - m2)
        den2 = jnp.sum(e2, axis=1, keepdims=True) + e_new
        os_full = _dot(e2.astype(BF16), buf[3].T.astype(BF16)) + e_new * new[:, KVD:2 * KVD]
        os_ref[s_i] = _pick_group_half(os_full) * (gates[1] / den2)

    @pl.when(step == 0)
    def _():
        start_fetch(0, buf_a, 0)

    start_fetch(2 * step + 1, buf_b, 1)
    wait_fetch(buf_a, 0)
    compute(buf_a, 0)

    @pl.when(step + 1 < n_steps)
    def _():
        start_fetch(2 * step + 2, buf_a, 0)

    wait_fetch(buf_b, 1)
    compute(buf_b, 1)


def _nsa_decode(q, new_sel, gate_logits2, pool, page_table, layer, cw, cmp_bias, sel_bias, bias0):
    db, n_pages = page_table.shape
    assert db % 2 == 0
    past = n_pages * PAGE_SIZE
    nch = past // CMP_STRIDE
    n_past = past // SEL_BLOCK
    n_blocks = n_past + 1
    nbp = -(-n_blocks // LANE) * LANE
    wc, w1f, pef, w2bd = cw
    ovl = jnp.asarray(_overlap_np(nch, n_blocks, nbp), BF16)
    ex = jnp.asarray((np.arange(n_past)[:, None] == (np.arange(past)[None, :] // SEL_BLOCK)).astype(np.float32), BF16)
    consts = [wc, w1f, pef, w2bd, cmp_bias, sel_bias, bias0, ovl, ex]
    per2 = lambda *tail: pl.BlockSpec((2,) + tail, lambda t, pt: (t,) + (0,) * len(tail))
    grid_spec = pltpu.PrefetchScalarGridSpec(
        num_scalar_prefetch=1, grid=(db // 2,),
        in_specs=[per2(N_HEADS, HEAD_DIM), per2(1, 2 * KVD), per2(2, N_HEADS, 1),
                  pl.BlockSpec(memory_space=pl.ANY)] + [_const_spec(c.shape) for c in consts],
        out_specs=[per2(N_HEADS, HEAD_DIM), per2(N_HEADS, HEAD_DIM)],
        scratch_shapes=[pltpu.VMEM((N_KIND, KVD, past), F32), pltpu.VMEM((N_KIND, KVD, past), F32),
                        pltpu.VMEM((past, KVD), F32), pltpu.VMEM((past, KVD), F32), pltpu.SemaphoreType.DMA((2,))])
    out = jax.ShapeDtypeStruct((db, N_HEADS, HEAD_DIM), F32)
    return pl.pallas_call(
        functools.partial(_nsa_decode_kernel, layer=layer, n_pages=n_pages, past=past,
                          n_blocks=n_blocks, nbp=nbp),
        grid_spec=grid_spec, out_shape=[out, out],
        compiler_params=_cparams("arbitrary"), name="nsa_decode")(
            page_table.reshape(-1), q, new_sel.reshape(db, 1, 2 * KVD), gate_logits2, pool, *consts)


def _shift_kernel(st_ref, new_ref, o_ref, *, lb):
    x = st_ref[...]
    lane = lax.broadcasted_iota(jnp.int32, x.shape, 2)
    o_ref[...] = jnp.where(lane == lb - 1, new_ref[...], pltpu.roll(x, lb - 1, 2))


def _shift_state(state, new_rows):
    db, w, lb = state.shape
    ns = max(1, min(db, (4 << 20) // (w * lb * 4)))
    assert db % ns == 0
    blk = pl.BlockSpec((ns, w, lb), lambda t: (t, 0, 0))
    return pl.pallas_call(
        functools.partial(_shift_kernel, lb=lb), grid=(db // ns,),
        in_specs=[blk, pl.BlockSpec((ns, w, 1), lambda t: (t, 0, 0))], out_specs=blk,
        out_shape=jax.ShapeDtypeStruct((db, w, lb), state.dtype),
        compiler_params=_cparams("parallel"), name="shift_state")(state, new_rows.reshape(db, w, 1))


def _decode_bias_row(rel_bias, past, kpos, window):
    dist = past - kpos
    valid = (dist >= 0) if window is None else ((dist >= 0) & (dist < window))
    return _bias_tile(rel_bias, dist[None, :], valid[None, :], False)[:, 0, :]


def kernel(x_prompt, x_sample, state_swa, cache_nsa, state_nsa_win, page_table, rel_bias, norm_mix, norm_mlp,
           norm_final, w_in_a, b_in_a, w_out_a, b_out_a, sinks_a, w_in_b, w_out_b, w_cmp1, w_cmp2, pe_cmp,
           w_up, w_down):
    n_batch, seq, _ = x_prompt.shape
    db = x_sample.shape[0]
    assert x_sample.shape[1] == 1
    depth = norm_mix.shape[0]
    n_la, n_lb = w_in_a.shape[0], w_in_b.shape[0]
    lba, lbb = state_swa.shape[1], state_nsa_win.shape[1]
    n_pages = page_table.shape[1]
    past = n_pages * PAGE_SIZE
    assert seq >= WIN_B and lba == WIN_A and lbb == WIN_B and past >= WIN_B

    kk = np.arange(Q_BLOCK)[:, None]
    qq = np.arange(Q_BLOCK)[None, :]
    d0, d1 = qq - kk, Q_BLOCK + qq - kk
    swa_tiles = LOG2E * jnp.stack([_bias_tile(rel_bias, d0, (d0 >= 0) & (d0 < WIN_A), False),
                                   _bias_tile(rel_bias, d1, (d1 >= 0) & (d1 < WIN_A), False)])
    near_tiles = LOG2E * jnp.stack([_bias_tile(rel_bias, d0, d0 >= 0, True),
                                    _bias_tile(rel_bias, d1, d1 >= 0, True)])
    nback_b = WIN_B // Q_BLOCK
    far_mask = jnp.asarray(np.where(kk > qq, 0.0, NEG), F32)
    no_mask = jnp.zeros((8, LANE), F32)
    cc = np.arange(NEAR_CMP)[:, None]
    dc0 = qq - CMP_STRIDE * cc - (CMP_LEN - 1)
    dc1 = qq - CMP_STRIDE * (cc - NEAR_CMP // 2) - (CMP_LEN - 1)
    cmp_tiles = LOG2E * jnp.stack([_bias_tile(rel_bias, dc0, dc0 >= 0, True),
                                   _bias_tile(rel_bias, dc1, dc1 >= 0, True)])
    nch_d = past // CMP_STRIDE
    cmp_row = _decode_bias_row(rel_bias, past, np.arange(nch_d) * CMP_STRIDE + CMP_LEN - 1, None)
    sel_row = _decode_bias_row(rel_bias, past, np.arange(past), None)
    swa_row = _decode_bias_row(rel_bias, past, past - lba + np.arange(lba), WIN_A)
    win_row = _decode_bias_row(rel_bias, past, past - lbb + np.arange(lbb), WIN_B)
    bias0 = rel_bias.astype(F32)[0].reshape(N_HEADS, 1)

    xp = x_prompt.reshape(n_batch * seq, D_MODEL)
    xs = x_sample.reshape(db, D_MODEL)
    to_minor = lambda a: jnp.transpose(a, (0, 2, 3, 4, 5, 1))
    st_swa = to_minor(state_swa).reshape(db, n_la, 2, KVD, lba)
    st_win = to_minor(state_nsa_win).reshape(db, n_lb, 2, KVD, lbb)
    pool = to_minor(cache_nsa).reshape(cache_nsa.shape[0], n_lb, N_KIND, KVD, PAGE_SIZE)
    swa_p, swa_new, rows_s, win_p, win_new = [], [], [], [], []
    rows_pt = jnp.zeros((n_batch, n_lb, N_KIND * KVD, seq), F32)

    for layer in range(depth):
        final = norm_final if layer == depth - 1 else None
        if layer % 2 == 0:
            a = layer // 2
            qt_p, kv_p, kvb_p, vt_p = _project(xp, norm_mix[layer], w_in_a[a], b_in_a[a], 2 * KVD, False, (KVD,))
            q_s, kv_s, _ = _project(xs, norm_mix[layer], w_in_a[a], b_in_a[a], 2 * KVD, False, None)
            o_p = _banded(qt_p, kvb_p, 0, vt_p, 0, n_batch, seq, ("bias", "bias"), swa_tiles, no_mask,
                          sinks_a[a], None, None)
            o_s = _window_decode(jnp.swapaxes(q_s, 0, 1), kv_s, st_swa, a, swa_row, bias0, sinks_a[a], None)
            swa_p.append(kv_p.reshape(n_batch, seq, 2, N_KV, HEAD_DIM)[:, seq - WIN_A:])
            swa_new.append(kv_s)
            xp = _out_mlp(xp, [o_p], w_out_a[a], b_out_a[a], norm_mlp[layer], w_up[layer], w_down[layer], final)
            xs = _out_mlp(xs, [o_s.reshape(db, HD)], w_out_a[a], b_out_a[a], norm_mlp[layer], w_up[layer],
                          w_down[layer], final)
        else:
            b = layer // 2
            cw = _compress_weights(w_cmp1[b], w_cmp2[b], pe_cmp[b])
            w_in = jnp.pad(w_in_b[b], ((0, 0), (0, LANE - N_GATE)))
            qt_p, kv_p, kvb_p, vt_p, glt_p, rows_pt = _project(xp, norm_mix[layer], w_in, None, 6 * KVD, True,
                                                               (3 * KVD, 5 * KVD), (rows_pt, b))
            q_s, kv_s, _, gl_s = _project(xs, norm_mix[layer], w_in, None, 6 * KVD, True, None)
            kc, vct = _compress_prompt(kv_p, n_batch, seq, cw)
            o_c, selt = _cmp_select(qt_p, kc, vct, cmp_tiles, glt_p, n_batch, seq)
            o_sel = _selected(qt_p, kvb_p, vt_p, 0, selt, near_tiles, glt_p, n_batch, seq)
            kinds = ("bias", "bias") + ("none",) * (nback_b - 2) + ("mask",)
            o_w = _banded(qt_p, kvb_p, 4, vt_p, 1, n_batch, seq, kinds, near_tiles, far_mask, None, glt_p, 2)
            win_p.append(kv_p.reshape(n_batch, seq, 6 * KVD)[:, seq - WIN_B:, 4 * KVD:]
                         .reshape(n_batch, WIN_B, 2, N_KV, HEAD_DIM))
            xp = _out_mlp(xp, [o_c, o_sel, o_w], w_out_b[b], None, norm_mlp[layer], w_up[layer], w_down[layer], final)
            q_sd = jnp.swapaxes(q_s, 0, 1)
            gl_s3 = gl_s.reshape(db, 3, N_HEADS, 1)
            oc_s, os_s = _nsa_decode(q_sd, kv_s[:, 2 * KVD:4 * KVD], gl_s3[:, 0:2], pool, page_table, b, cw,
                                     cmp_row, sel_row, bias0)
            ow_s = _window_decode(q_sd, kv_s[:, 4 * KVD:], st_win, b, win_row, bias0, None, gl_s3[:, 2])
            rows_s.append(kv_s[:, :4 * KVD].reshape(db, 1, 4, N_KV, HEAD_DIM))
            win_new.append(kv_s[:, 4 * KVD:])
            xs = _out_mlp(xs, [oc_s.reshape(db, HD), os_s.reshape(db, HD), ow_s.reshape(db, HD)], w_out_b[b], None,
                          norm_mlp[layer], w_up[layer], w_down[layer], final)

    y_prompt = xp.reshape(n_batch, seq, D_MODEL)
    y_sample = xs.reshape(db, 1, D_MODEL)
    from_minor = lambda a, nl, lb: jnp.transpose(a.reshape(db, nl, 2, N_KV, HEAD_DIM, lb), (0, 5, 1, 2, 3, 4))
    swa_sample = _shift_state(st_swa.reshape(db, n_la * 2 * KVD, lba), jnp.concatenate(swa_new, axis=1))
    win_sample = _shift_state(st_win.reshape(db, n_lb * 2 * KVD, lbb), jnp.concatenate(win_new, axis=1))
    return (y_prompt, y_sample,
            jnp.stack(swa_p, axis=2),
            from_minor(swa_sample, n_la, lba),
            jnp.transpose(rows_pt.reshape(n_batch, n_lb, N_KIND, N_KV, HEAD_DIM, seq), (0, 5, 1, 2, 3, 4)),
            jnp.stack(rows_s, axis=2),
            jnp.stack(win_p, axis=2),
            from_minor(win_sample, n_lb, lbb))
```
